```python
import math
import jax, jax.numpy as jnp
from jax import lax
import numpy as np

D_MODEL = 2048
BATCH = 1
SEQ = 8192
DEPTH = 2
DEC_BATCH = 16
DEC_SEQ = 2048
PAST_LEN = 128

PLE_DIM = 256
GRID_W = 64
QBLK = 128
N_BRANCH = 4
H_A = 4
DK_A = 64
DV_A = 128
H_B = 4
Q_RANK_B = 512
KV_RANK_B = 512
DN_B = 128
DR_B = 64
DV_B = 128
H_C = 4
KV_C = 2
DH_C = 128
H_D = 4
DH_D = 128
DILATIONS = ((128, 1), (512, 4), (2048, 16))
N_DIL = 3
BRANCH_W = 512
D_FF = 4 * D_MODEL
N_BUCKETS = 32
REL_MAX_DIST = 1024
N_BIAS_HEADS = H_A + N_DIL * H_D
ROPE_THETA = 10000.0
NORM_EPS = 1e-6
NEG_INF = -1e30
IN_SIZES = (H_A * 2 * DK_A, H_A * 2 * DK_A, H_A * DV_A,
            Q_RANK_B, KV_RANK_B, DR_B,
            H_C * DH_C, KV_C * DH_C, KV_C * DH_C,
            N_DIL * H_D * DH_D, H_D * DH_D, H_D * DH_D)
IN_SPLITS = tuple(int(s) for s in np.cumsum(IN_SIZES)[:-1])
IN_WIDTH = int(sum(IN_SIZES))

kernel_name = 'gated_hybrid_bidir_encoder'


def rms_norm(x, g):
    xf = x.astype(jnp.float32)
    y = xf * lax.rsqrt(jnp.mean(xf * xf, axis=-1, keepdims=True) + NORM_EPS)
    return (y * g).astype(x.dtype)


def rel_bucket(rel):
    half = N_BUCKETS // 2
    exact = half // 2
    n = jnp.abs(rel)
    nf = jnp.maximum(n, 1).astype(jnp.float32)
    large = exact + (jnp.log(nf / exact) / math.log(REL_MAX_DIST / exact) * (half - exact)).astype(jnp.int32)
    large = jnp.minimum(large, half - 1)
    return jnp.where(rel > 0, half, 0) + jnp.where(n < exact, n, large)


def rope(x, pos):
    d = x.shape[-1]
    half = d // 2
    inv = jnp.power(ROPE_THETA, -2.0 * jnp.arange(half, dtype=jnp.float32) / d)
    ang = pos[:, None] * inv[None, :]
    cos = jnp.cos(ang)[:, None, :]
    sin = jnp.sin(ang)[:, None, :]
    x1 = x[..., :half].astype(jnp.float32)
    x2 = x[..., half:].astype(jnp.float32)
    return jnp.concatenate([x1 * cos - x2 * sin, x2 * cos + x1 * sin], axis=-1).astype(x.dtype)


def axial_rope(x, row, col):
    half = x.shape[-1] // 2
    return jnp.concatenate([rope(x[..., :half], row), rope(x[..., half:], col)], axis=-1)


def dense_attention(q, k, v, bias_tbl, scale):
    B, S, Hk, G, dk = q.shape
    T = k.shape[1]
    nb = S // QBLK
    qb = q.reshape(B, nb, QBLK, Hk, G, dk).transpose(1, 0, 2, 3, 4, 5)
    kpos = jnp.arange(T, dtype=jnp.int32)

    def one(args):
        qblk, idx = args
        s = jnp.einsum('bqhgd,bkhd->bhgqk', qblk, k, preferred_element_type=jnp.float32) * scale
        if bias_tbl is not None:
            qpos = idx * QBLK + jnp.arange(QBLK, dtype=jnp.int32)
            bucket = rel_bucket(kpos[None, :] - qpos[:, None])
            s = s + bias_tbl[bucket].transpose(2, 3, 0, 1)
        p = jax.nn.softmax(s, axis=-1)
        return jnp.einsum('bhgqk,bkhd->bqhgd', p.astype(v.dtype), v)

    o = lax.map(one, (qb, jnp.arange(nb, dtype=jnp.int32)))
    return o.transpose(1, 0, 2, 3, 4, 5).reshape(B, S, Hk, G, v.shape[-1])


def dilated_band_attention(q, k, v, window, dil, bias_tbl, scale):
    B, S, H, d = q.shape
    half = window // (2 * dil)
    L = S // dil
    nb = -(-L // half)
    Lp = nb * half

    def to_sub(x):
        return x.reshape(B, L, dil, H, x.shape[-1]).transpose(0, 2, 1, 3, 4)

    qs = jnp.pad(to_sub(q), ((0, 0), (0, 0), (0, Lp - L), (0, 0), (0, 0))).reshape(B, dil, nb, half, H, d)

    def key_blocks(x):
        xp = jnp.pad(to_sub(x), ((0, 0), (0, 0), (half, Lp - L + half), (0, 0), (0, 0)))
        xb = xp.reshape(B, dil, nb + 2, half, H, x.shape[-1])
        return jnp.concatenate([xb[:, :, :-2], xb[:, :, 1:-1], xb[:, :, 2:]], axis=3)

    kb = key_blocks(k)
    vb = key_blocks(v)
    s = jnp.einsum('brnqhd,brnkhd->brnhqk', qs, kb, preferred_element_type=jnp.float32) * scale
    qi = jnp.arange(half, dtype=jnp.int32)[:, None]
    kj = jnp.arange(3 * half, dtype=jnp.int32)[None, :]
    rel = kj - half - qi
    bias = bias_tbl[rel_bucket(rel * dil)].transpose(2, 0, 1)
    kpos = (jnp.arange(nb, dtype=jnp.int32)[:, None] - 1) * half + kj
    valid = (kpos >= 0) & (kpos < L)
    mask = (jnp.abs(rel) <= half)[None] & valid[:, None, :]
    s = jnp.where(mask[:, None], s + bias, NEG_INF)
    lse = jax.nn.logsumexp(s, axis=-1)
    p = jnp.exp(s - lse[..., None])
    o = jnp.einsum('brnhqk,brnkhd->brnqhd', p.astype(v.dtype), vb)
    o = o.reshape(B, dil, Lp, H, d)[:, :, :L].transpose(0, 2, 1, 3, 4).reshape(B, S, H, d)
    lse = lse.transpose(0, 1, 2, 4, 3).reshape(B, dil, Lp, H)[:, :, :L].transpose(0, 2, 1, 3).reshape(B, S, H)
    return o, lse


def diff_attention(aq, ak, av, q_g, k_g, lq1, lk1, lq2, lk2, out_g, bias_tbl, lam_init):
    B, S, _ = aq.shape
    q = rms_norm(aq.reshape(B, S, H_A, 2, DK_A), q_g)
    k = rms_norm(ak.reshape(B, S, H_A, 2, DK_A), k_g)
    v = av.reshape(B, S, H_A, DV_A)
    tbl = bias_tbl[:, :, None]
    o1 = dense_attention(q[:, :, :, 0, None], k[:, :, :, 0], v, tbl, DK_A ** -0.5)
    o2 = dense_attention(q[:, :, :, 1, None], k[:, :, :, 1], v, tbl, DK_A ** -0.5)
    lam = (jnp.exp(jnp.sum(lq1.astype(jnp.float32) * lk1.astype(jnp.float32)))
           - jnp.exp(jnp.sum(lq2.astype(jnp.float32) * lk2.astype(jnp.float32))) + lam_init)
    o = (o1 - lam.astype(o1.dtype) * o2)[:, :, :, 0]
    o = rms_norm(o, out_g) * (1.0 - lam_init)
    return o.reshape(B, S, H_A * DV_A)


def latent_attention(cq, ckv, kr, cq_g, ckv_g, w_uq, w_ukv, q_g, k_g, pos):
    B, S, _ = cq.shape
    q = (rms_norm(cq, cq_g) @ w_uq).reshape(B, S, H_B, DN_B + DR_B)
    kv = (rms_norm(ckv, ckv_g) @ w_ukv).reshape(B, S, H_B, DN_B + DV_B)
    k_nope = kv[..., :DN_B]
    v = kv[..., DN_B:]
    k = jnp.concatenate([k_nope, jnp.broadcast_to(kr[:, :, None, :], (B, S, H_B, DR_B))], axis=-1)
    q = rms_norm(q, q_g)
    k = rms_norm(k, k_g)
    q = jnp.concatenate([q[..., :DN_B], rope(q[..., DN_B:], pos)], axis=-1)
    k = jnp.concatenate([k[..., :DN_B], rope(k[..., DN_B:], pos)], axis=-1)
    o = dense_attention(q[:, :, :, None], k, v, None, (DN_B + DR_B) ** -0.5)
    return o.reshape(B, S, H_B * DV_B)


def axial_gqa(cq, ck, cv, q_g, k_g, row, col):
    B, S, _ = cq.shape
    q = axial_rope(rms_norm(cq.reshape(B, S, H_C, DH_C), q_g), row, col)
    k = axial_rope(rms_norm(ck.reshape(B, S, KV_C, DH_C), k_g), row, col)
    v = cv.reshape(B, S, KV_C, DH_C)
    o = dense_attention(q.reshape(B, S, KV_C, H_C // KV_C, DH_C), k, v, None, DH_C ** -0.5)
    return o.reshape(B, S, H_C * DH_C)


def dilated_mixture(dq, dk, dv, q_g, k_g, bias_tbl):
    B, S, _ = dq.shape
    q = rms_norm(dq.reshape(B, S, N_DIL, H_D, DH_D), q_g)
    k = rms_norm(dk.reshape(B, S, H_D, DH_D), k_g)
    v = dv.reshape(B, S, H_D, DH_D)
    outs = []
    lses = []
    for g, (window, dil) in enumerate(DILATIONS):
        o, l = dilated_band_attention(q[:, :, g], k, v, window, dil,
                                      bias_tbl[:, g * H_D:(g + 1) * H_D], DH_D ** -0.5)
        outs.append(o)
        lses.append(l)
    w = jax.nn.softmax(jnp.stack(lses), axis=0)
    o = jnp.einsum('gbsh,gbshd->bshd', w.astype(v.dtype), jnp.stack(outs))
    return o.reshape(B, S, H_D * DH_D)


def setup_inputs(seed: int = 0) -> dict:
    key = jax.random.key(seed)
    ks = iter(jax.random.split(key, 40))
    f32 = jnp.float32

    def nrm(shape, scale):
        return jax.random.normal(next(ks), shape, f32) * scale

    def gain(shape):
        return 1.0 + 0.1 * jax.random.normal(next(ks), shape, f32)

    return {
        'x_prompt': nrm((BATCH, SEQ, D_MODEL), 1.0),
        'x_sample': nrm((DEC_BATCH, DEC_SEQ, D_MODEL), 1.0),
        'p_prompt': nrm((DEPTH, BATCH, SEQ, PLE_DIM), 1.0),
        'p_sample': nrm((DEPTH, DEC_BATCH, DEC_SEQ, PLE_DIM), 1.0),
        'rel_bias': nrm((N_BUCKETS, N_BIAS_HEADS), 0.5),
        'norm_mix': gain((DEPTH, D_MODEL)),
        'w_in': nrm((DEPTH, D_MODEL, IN_WIDTH), D_MODEL ** -0.5),
        'a_q_norm': gain((DEPTH, DK_A)),
        'a_k_norm': gain((DEPTH, DK_A)),
        'a_lambda_q1': nrm((DEPTH, DK_A), 0.1),
        'a_lambda_k1': nrm((DEPTH, DK_A), 0.1),
        'a_lambda_q2': nrm((DEPTH, DK_A), 0.1),
        'a_lambda_k2': nrm((DEPTH, DK_A), 0.1),
        'a_out_norm': gain((DEPTH, DV_A)),
        'b_cq_norm': gain((DEPTH, Q_RANK_B)),
        'b_ckv_norm': gain((DEPTH, KV_RANK_B)),
        'b_w_uq': nrm((DEPTH, Q_RANK_B, H_B * (DN_B + DR_B)), Q_RANK_B ** -0.5),
        'b_w_ukv': nrm((DEPTH, KV_RANK_B, H_B * (DN_B + DV_B)), KV_RANK_B ** -0.5),
        'b_q_norm': gain((DEPTH, DN_B + DR_B)),
        'b_k_norm': gain((DEPTH, DN_B + DR_B)),
        'c_q_norm': gain((DEPTH, DH_C)),
        'c_k_norm': gain((DEPTH, DH_C)),
        'd_q_norm': gain((DEPTH, DH_D)),
        'd_k_norm': gain((DEPTH, DH_D)),
        'w_gate': nrm((DEPTH, N_BRANCH, D_MODEL, D_MODEL), D_MODEL ** -0.5),
        'w_branch': nrm((DEPTH, N_BRANCH, BRANCH_W, D_MODEL), BRANCH_W ** -0.5),
        'w_out': nrm((DEPTH, D_MODEL, D_MODEL), D_MODEL ** -0.5),
        'norm_ffn': gain((DEPTH, D_MODEL)),
        'w_ff1': nrm((DEPTH, D_MODEL, D_FF), D_MODEL ** -0.5),
        'w_ff2': nrm((DEPTH, D_FF, D_MODEL), D_FF ** -0.5),
        'norm_ple': gain((DEPTH, D_MODEL)),
        'w_ple_gate': nrm((DEPTH, D_MODEL, D_MODEL), D_MODEL ** -0.5),
        'w_ple_proj': nrm((DEPTH, PLE_DIM, D_MODEL), PLE_DIM ** -0.5),
    }


def reference(x_prompt, x_sample, p_prompt, p_sample, rel_bias, norm_mix, w_in,
              a_q_norm, a_k_norm, a_lambda_q1, a_lambda_k1, a_lambda_q2, a_lambda_k2, a_out_norm,
              b_cq_norm, b_ckv_norm, b_w_uq, b_w_ukv, b_q_norm, b_k_norm,
              c_q_norm, c_k_norm, d_q_norm, d_k_norm,
              w_gate, w_branch, w_out, norm_ffn, w_ff1, w_ff2,
              norm_ple, w_ple_gate, w_ple_proj):
    bias_a = rel_bias[:, :H_A]
    bias_d = rel_bias[:, H_A:]

    def trunk(x, p):
        B, S, _ = x.shape
        rows = S // GRID_W
        pos = jnp.arange(S, dtype=jnp.float32)
        row = jnp.repeat(jnp.arange(rows, dtype=jnp.float32), GRID_W)
        col = jnp.tile(jnp.arange(GRID_W, dtype=jnp.float32), rows)
        for i in range(DEPTH):
            lam_init = 0.8 - 0.6 * math.exp(-0.3 * i)
            h = rms_norm(x, norm_mix[i])
            (aq, ak, av, bcq, bckv, bkr, cq, ck, cv, dq, dk, dv) = jnp.split(h @ w_in[i], IN_SPLITS, axis=-1)
            branches = (
                diff_attention(aq, ak, av, a_q_norm[i], a_k_norm[i], a_lambda_q1[i], a_lambda_k1[i],
                               a_lambda_q2[i], a_lambda_k2[i], a_out_norm[i], bias_a, lam_init),
                latent_attention(bcq, bckv, bkr, b_cq_norm[i], b_ckv_norm[i], b_w_uq[i], b_w_ukv[i],
                                 b_q_norm[i], b_k_norm[i], pos),
                axial_gqa(cq, ck, cv, c_q_norm[i], c_k_norm[i], row, col),
                dilated_mixture(dq, dk, dv, d_q_norm[i], d_k_norm[i], bias_d),
            )
            merged = jax.nn.sigmoid(h @ w_gate[i, 0]) * (branches[0] @ w_branch[i, 0])
            for b in range(1, N_BRANCH):
                merged = merged + jax.nn.sigmoid(h @ w_gate[i, b]) * (branches[b] @ w_branch[i, b])
            x = x + merged @ w_out[i]
            h2 = rms_norm(x, norm_ffn[i])
            x = x + jnp.square(jax.nn.relu(h2 @ w_ff1[i])) @ w_ff2[i]
            h3 = rms_norm(x, norm_ple[i])
            x = x + jax.nn.sigmoid(h3 @ w_ple_gate[i]) * (p[i] @ w_ple_proj[i])
        return x

    y_prompt = trunk(x_prompt, p_prompt)
    y_sample = trunk(x_sample, p_sample)
    return (y_prompt, y_sample)
```

```python
import functools
import math

import jax
import jax.numpy as jnp
import numpy as np
from jax import lax
from jax.experimental import pallas as pl
from jax.experimental.pallas import tpu as pltpu

BF16 = jnp.bfloat16
F32 = jnp.float32

GRID_W = 64
H_A, DK_A, DV_A = 4, 64, 128
H_B, DN_B, DR_B, DV_B = 4, 128, 64, 128
H_C, KV_C, DH_C = 4, 2, 128
H_D, DH_D = 4, 128
DILATIONS = ((128, 1), (512, 4), (2048, 16))
N_DIL = len(DILATIONS)
N_BUCKETS = 32
REL_MAX_DIST = 1024
ROPE_THETA = 10000.0
NORM_EPS = 1e-6
NEG_INF = -1e30
LANES = 128
DKP_B = 256
VMEM_LIMIT = 56 * 1024 * 1024


def _cparams(sem):
    return pltpu.CompilerParams(dimension_semantics=sem, vmem_limit_bytes=VMEM_LIMIT)


def _pick(n, prefs):
    for t in prefs:
        if n % t == 0:
            return t
    return n


def _rmsnorm_body(x_ref, g_ref, o_ref):
    x = x_ref[...]
    y = x * lax.rsqrt(jnp.mean(x * x, axis=-1, keepdims=True) + NORM_EPS)
    o_ref[...] = (y * g_ref[...]).astype(o_ref.dtype)


def rmsnorm(x, g, out_dtype=BF16):
    m, d = x.shape
    tm = _pick(m, (512, 256, 128, 64, 8))
    return pl.pallas_call(
        _rmsnorm_body,
        out_shape=jax.ShapeDtypeStruct((m, d), out_dtype),
        grid=(m // tm,),
        in_specs=[pl.BlockSpec((tm, d), lambda i: (i, 0)),
                  pl.BlockSpec((1, d), lambda i: (0, 0))],
        out_specs=pl.BlockSpec((tm, d), lambda i: (i, 0)),
        compiler_params=_cparams(("parallel",)),
        name="rmsnorm",
    )(x, g.reshape(1, d).astype(F32))


def _mm_body(*refs, act, has_res, nk):
    if has_res:
        a_ref, w_ref, r_ref, o_ref = refs[:4]
        scratch = refs[4:]
    else:
        a_ref, w_ref, o_ref = refs[:3]
        r_ref = None
        scratch = refs[3:]

    def finish(acc):
        if act == "relu2":
            acc = jnp.square(jnp.maximum(acc, 0.0))
        if has_res:
            acc = r_ref[...] + acc
        o_ref[...] = acc.astype(o_ref.dtype)

    part = jnp.dot(a_ref[...], w_ref[...], preferred_element_type=F32)
    if nk == 1:
        finish(part)
        return
    acc_ref, = scratch
    k = pl.program_id(2)

    @pl.when(k == 0)
    def _():
        acc_ref[...] = part

    @pl.when(k > 0)
    def _():
        acc_ref[...] += part

    @pl.when(k == nk - 1)
    def _():
        finish(acc_ref[...])


def matmul(a, w, *, out_dtype, act=None, res=None, tm=1024, tn=512, tk=2048):
    m, kdim = a.shape
    n = w.shape[1]
    tm = _pick(m, (tm, 512, 256, 128, 64, 8))
    tn = _pick(n, (tn, 256, 128))
    tk = _pick(kdim, (tk, 1024, 512))
    nk = kdim // tk
    in_specs = [pl.BlockSpec((tm, tk), lambda i, j, k: (i, k)),
                pl.BlockSpec((tk, tn), lambda i, j, k: (k, j))]
    args = [a, w]
    if res is not None:
        in_specs.append(pl.BlockSpec((tm, tn), lambda i, j, k: (i, j)))
        args.append(res)
    return pl.pallas_call(
        functools.partial(_mm_body, act=act, has_res=res is not None, nk=nk),
        out_shape=jax.ShapeDtypeStruct((m, n), out_dtype),
        grid=(m // tm, n // tn, nk),
        in_specs=in_specs,
        out_specs=pl.BlockSpec((tm, tn), lambda i, j, k: (i, j)),
        scratch_shapes=[pltpu.VMEM((tm, tn), F32)] if nk > 1 else [],
        compiler_params=_cparams(("parallel", "parallel", "arbitrary")),
        name="matmul",
    )(*args)


def _gated_body(*refs, nb, has_res):
    h_ref, wg_ref, wb_ref = refs[:3]
    br_refs = refs[3:3 + nb]
    rest = refs[3 + nb:]
    if has_res:
        r_ref, o_ref = rest
    else:
        o_ref, = rest
    h = h_ref[...]
    acc = None
    for b in range(nb):
        gate = jnp.dot(h, wg_ref[b], preferred_element_type=F32)
        val = jnp.dot(br_refs[b][...].astype(BF16), wb_ref[b], preferred_element_type=F32)
        term = jax.nn.sigmoid(gate) * val
        acc = term if acc is None else acc + term
    if has_res:
        acc = r_ref[...] + acc
    o_ref[...] = acc.astype(o_ref.dtype)


def gated_merge(h, wg, wb, branches, *, out_dtype, res=None, tm=1024, tn=256):
    m, d = h.shape
    nb, _, n = wg.shape
    kb = wb.shape[1]
    tm = _pick(m, (tm, 512, 256, 128, 64, 8))
    tn = _pick(n, (tn, 128))
    in_specs = [pl.BlockSpec((tm, d), lambda i, j: (i, 0)),
                pl.BlockSpec((nb, d, tn), lambda i, j: (0, 0, j)),
                pl.BlockSpec((nb, kb, tn), lambda i, j: (0, 0, j))]
    in_specs += [pl.BlockSpec((tm, kb), lambda i, j: (i, 0)) for _ in range(nb)]
    args = [h, wg, wb, *branches]
    if res is not None:
        in_specs.append(pl.BlockSpec((tm, tn), lambda i, j: (i, j)))
        args.append(res)
    return pl.pallas_call(
        functools.partial(_gated_body, nb=nb, has_res=res is not None),
        out_shape=jax.ShapeDtypeStruct((m, n), out_dtype),
        grid=(m // tm, n // tn),
        in_specs=in_specs,
        out_specs=pl.BlockSpec((tm, tn), lambda i, j: (i, j)),
        compiler_params=_cparams(("parallel", "parallel")),
        name="gated_merge",
    )(*args)


def _softmax_step(s, v, m_ref, l_ref, acc_ref):
    m_prev = m_ref[...]
    m_new = jnp.maximum(m_prev, jnp.max(s, axis=1, keepdims=True))
    alpha = jnp.exp(m_prev - m_new)
    p = jnp.exp(s - m_new)
    l_ref[...] = alpha * l_ref[...] + jnp.sum(p, axis=1, keepdims=True)
    acc_ref[...] = alpha * acc_ref[...] + jnp.dot(p.astype(BF16), v, preferred_element_type=F32)
    m_ref[...] = m_new


def _qk(q, k):
    return lax.dot_general(q, k, (((1,), (1,)), ((), ())), preferred_element_type=F32)


def _attn_body(q_ref, k_ref, v_ref, o_ref, m_ref, l_ref, acc_ref, *, tk, nk):
    m_ref[...] = jnp.full(m_ref.shape, -jnp.inf, F32)
    l_ref[...] = jnp.zeros(l_ref.shape, F32)
    acc_ref[...] = jnp.zeros(acc_ref.shape, F32)
    q = q_ref[...]

    def step(j, carry):
        off = pl.multiple_of(j * tk, tk)
        s = _qk(q, k_ref[pl.ds(off, tk), :])
        _softmax_step(s, v_ref[pl.ds(off, tk), :], m_ref, l_ref, acc_ref)
        return carry

    lax.fori_loop(0, nk, step, 0)
    o_ref[...] = (acc_ref[...] / l_ref[...]).astype(o_ref.dtype)


def dense_attention(q, k, v, *, row0, nseq, seq, hq, hk, dk, dv):
    tq = _pick(seq, (512, 256, 128))
    tk = _pick(seq, (512, 256, 128))
    nq = seq // tq
    group = hq // hk
    qb0 = row0 // tq
    sb0 = row0 // seq
    return pl.pallas_call(
        functools.partial(_attn_body, tk=tk, nk=seq // tk),
        out_shape=jax.ShapeDtypeStruct((nseq * seq, hq * dv), BF16),
        grid=(nseq, hq, nq),
        in_specs=[pl.BlockSpec((tq, dk), lambda b, h, i: (qb0 + b * nq + i, h)),
                  pl.BlockSpec((seq, dk), lambda b, h, i: (sb0 + b, h // group)),
                  pl.BlockSpec((seq, dv), lambda b, h, i: (sb0 + b, h // group))],
        out_specs=pl.BlockSpec((tq, dv), lambda b, h, i: (b * nq + i, h)),
        scratch_shapes=[pltpu.VMEM((tq, 1), F32), pltpu.VMEM((tq, 1), F32),
                        pltpu.VMEM((tq, dv), F32)],
        compiler_params=_cparams(("parallel", "parallel", "parallel")),
        name="dense_attention",
    )(q, k, v)


def _diff_body(lam_ref, q_ref, k_ref, v_ref, bias_ref, g_ref, o_ref,
               m1, l1, a1, m2, l2, a2, *, tk, nk, rmax, out_scale):
    for m_ref, l_ref, a_ref in ((m1, l1, a1), (m2, l2, a2)):
        m_ref[...] = jnp.full(m_ref.shape, -jnp.inf, F32)
        l_ref[...] = jnp.zeros(l_ref.shape, F32)
        a_ref[...] = jnp.zeros(a_ref.shape, F32)
    q = q_ref[...]
    lane = lax.broadcasted_iota(jnp.int32, q.shape, 1)
    zero = jnp.zeros_like(q)
    q1 = jnp.where(lane < DK_A, q, zero)
    q2 = jnp.where(lane >= DK_A, q, zero)
    qi = pl.program_id(2)

    def step(j, carry):
        off = pl.multiple_of(j * tk, tk)
        k = k_ref[pl.ds(off, tk), :]
        v = v_ref[pl.ds(off, tk), :]
        bias = bias_ref[jnp.clip(j - qi, -rmax, rmax) + rmax]
        _softmax_step(_qk(q1, k) + bias, v, m1, l1, a1)
        _softmax_step(_qk(q2, k) + bias, v, m2, l2, a2)
        return carry

    lax.fori_loop(0, nk, step, 0)
    o = a1[...] / l1[...] - lam_ref[0] * (a2[...] / l2[...])
    y = o * lax.rsqrt(jnp.mean(o * o, axis=-1, keepdims=True) + NORM_EPS)
    o_ref[...] = ((y * g_ref[...]) * out_scale).astype(o_ref.dtype)


def _rel_bucket(rel):
    half = N_BUCKETS // 2
    exact = half // 2
    n = jnp.abs(rel)
    nf = jnp.maximum(n, 1).astype(F32)
    large = exact + (jnp.log(nf / exact) / math.log(REL_MAX_DIST / exact) * (half - exact)).astype(jnp.int32)
    large = jnp.minimum(large, half - 1)
    return jnp.where(rel > 0, half, 0) + jnp.where(n < exact, n, large)


def _diff_bias_tiles(bias_a, t):
    rmax = -(-(REL_MAX_DIST - 1) // t) + 1
    db = jnp.arange(-rmax, rmax + 1, dtype=jnp.int32)[:, None, None]
    qi = jnp.arange(t, dtype=jnp.int32)[None, :, None]
    kj = jnp.arange(t, dtype=jnp.int32)[None, None, :]
    bucket = _rel_bucket(db * t + kj - qi)
    return jnp.transpose(bias_a[bucket], (3, 0, 1, 2)).astype(F32), rmax


def diff_attention(q, k, v, bias_tiles, rmax, lam, out_g, *, row0, nseq, seq, lam_init):
    t = bias_tiles.shape[-1]
    nq = seq // t
    qb0 = row0 // t
    sb0 = row0 // seq
    nt = 2 * rmax + 1
    return pl.pallas_call(
        functools.partial(_diff_body, tk=t, nk=nq, rmax=rmax, out_scale=1.0 - lam_init),
        out_shape=jax.ShapeDtypeStruct((nseq * seq, H_A * DV_A), BF16),
        grid=(nseq, H_A, nq),
        in_specs=[pl.BlockSpec(memory_space=pltpu.SMEM),
                  pl.BlockSpec((t, 2 * DK_A), lambda b, h, i: (qb0 + b * nq + i, h)),
                  pl.BlockSpec((seq, 2 * DK_A), lambda b, h, i: (sb0 + b, h)),
                  pl.BlockSpec((seq, DV_A), lambda b, h, i: (sb0 + b, h)),
                  pl.BlockSpec((None, nt, t, t), lambda b, h, i: (h, 0, 0, 0)),
                  pl.BlockSpec((1, DV_A), lambda b, h, i: (0, 0))],
        out_specs=pl.BlockSpec((t, DV_A), lambda b, h, i: (b * nq + i, h)),
        scratch_shapes=[pltpu.VMEM((t, 1), F32), pltpu.VMEM((t, 1), F32), pltpu.VMEM((t, DV_A), F32),
                        pltpu.VMEM((t, 1), F32), pltpu.VMEM((t, 1), F32), pltpu.VMEM((t, DV_A), F32)],
        compiler_params=_cparams(("parallel", "parallel", "parallel")),
        name="diff_attention",
    )(lam, q, k, v, bias_tiles, out_g.reshape(1, DV_A).astype(F32))


def _band_body(q_ref, kp_ref, kc_ref, kn_ref, vp_ref, vc_ref, vn_ref, bias_ref, o_ref, lse_ref,
               *, tu, length):
    n = pl.program_id(3)
    k = jnp.concatenate([kp_ref[...], kc_ref[...], kn_ref[...]], axis=0)
    v = jnp.concatenate([vp_ref[...], vc_ref[...], vn_ref[...]], axis=0)
    s = _qk(q_ref[...], k) + bias_ref[...]
    ku = (n - 1) * tu + lax.broadcasted_iota(jnp.int32, s.shape, 1)
    s = jnp.where((ku >= 0) & (ku < length), s, NEG_INF)
    m = jnp.max(s, axis=1, keepdims=True)
    p = jnp.exp(s - m)
    l = jnp.sum(p, axis=1, keepdims=True)
    o = jnp.dot(p.astype(BF16), v, preferred_element_type=F32) / l
    o_ref[...] = o.astype(o_ref.dtype)
    lse_ref[...] = jnp.broadcast_to(m + jnp.log(l), lse_ref.shape)


def _band_bias_tiles(bias_d, tu):
    tiles = []
    qi = jnp.arange(tu, dtype=jnp.int32)[:, None]
    kj = jnp.arange(3 * tu, dtype=jnp.int32)[None, :]
    rel = kj - tu - qi
    for g, (window, dil) in enumerate(DILATIONS):
        half = window // (2 * dil)
        tbl = bias_d[:, g * H_D:(g + 1) * H_D]
        bias = jnp.transpose(tbl[_rel_bucket(rel * dil)], (2, 0, 1))
        tiles.append(jnp.where((jnp.abs(rel) <= half)[None], bias, NEG_INF))
    return jnp.stack(tiles).astype(F32)


def band_attention(q, k, v, bias_tiles, g, *, row0, nseq, seq):
    _, dil = DILATIONS[g]
    length = seq // dil
    tu = bias_tiles.shape[2]
    nb = length // tu
    m = q.shape[0]
    qv = q.reshape(m // dil, dil * N_DIL * H_D * DH_D)
    kv = k.reshape(m // dil, dil * H_D * DH_D)
    vv = v.reshape(m // dil, dil * H_D * DH_D)
    ub0 = row0 // dil // tu

    def rows(b, n):
        return ub0 + b * nb + n

    def kspec(shift):
        return pl.BlockSpec(
            (tu, DH_D),
            lambda b, r, h, n: (rows(b, jnp.clip(n + shift, 0, nb - 1)), r * H_D + h))

    mo = nseq * seq
    out_spec = pl.BlockSpec((tu, DH_D), lambda b, r, h, n: (b * nb + n, r * H_D + h))
    o, lse = pl.pallas_call(
        functools.partial(_band_body, tu=tu, length=length),
        out_shape=(jax.ShapeDtypeStruct((mo // dil, dil * H_D * DH_D), F32),
                   jax.ShapeDtypeStruct((mo // dil, dil * H_D * DH_D), F32)),
        grid=(nseq, dil, H_D, nb),
        in_specs=[pl.BlockSpec((tu, DH_D), lambda b, r, h, n: (rows(b, n), (r * N_DIL + g) * H_D + h)),
                  kspec(-1), kspec(0), kspec(1), kspec(-1), kspec(0), kspec(1),
                  pl.BlockSpec((None, None, tu, 3 * tu), lambda b, r, h, n: (g, h, 0, 0))],
        out_specs=(out_spec, out_spec),
        compiler_params=_cparams(("parallel", "parallel", "parallel", "parallel")),
        name="band_attention",
    )(qv, kv, kv, kv, vv, vv, vv, bias_tiles)
    return o.reshape(mo, H_D * DH_D), lse.reshape(mo, H_D * DH_D)


def _mix_body(*refs):
    o_refs = refs[:N_DIL]
    l_refs = refs[N_DIL:2 * N_DIL]
    out_ref = refs[2 * N_DIL]
    ls = [r[...] for r in l_refs]
    mx = functools.reduce(jnp.maximum, ls)
    es = [jnp.exp(l - mx) for l in ls]
    den = functools.reduce(lambda a, b: a + b, es)
    acc = None
    for e, o_ref in zip(es, o_refs):
        term = (e / den) * o_ref[...]
        acc = term if acc is None else acc + term
    out_ref[...] = acc.astype(out_ref.dtype)


def band_mixture(outs, lses):
    m, c = outs[0].shape
    tm = _pick(m, (1024, 512, 256, 128, 64, 8))
    spec = pl.BlockSpec((tm, c), lambda i: (i, 0))
    return pl.pallas_call(
        _mix_body,
        out_shape=jax.ShapeDtypeStruct((m, c), BF16),
        grid=(m // tm,),
        in_specs=[spec] * (2 * N_DIL),
        out_specs=spec,
        compiler_params=_cparams(("parallel",)),
        name="band_mixture",
    )(*outs, *lses)


def _norm(x, g):
    return x * lax.rsqrt(jnp.mean(x * x, axis=-1, keepdims=True) + NORM_EPS) * g


def _rope_tables(pos, d):
    half = d // 2
    inv = jnp.power(ROPE_THETA, -2.0 * jnp.arange(half, dtype=F32) / d)
    ang = pos[:, None] * inv[None, :]
    return jnp.cos(ang)[:, None, :], jnp.sin(ang)[:, None, :]


def _rope(x, cs):
    cos, sin = cs
    half = x.shape[-1] // 2
    x1, x2 = x[..., :half], x[..., half:]
    return jnp.concatenate([x1 * cos - x2 * sin, x2 * cos + x1 * sin], axis=-1)


def kernel(x_prompt, x_sample, p_prompt, p_sample, rel_bias, norm_mix, w_in, a_q_norm, a_k_norm, a_lambda_q1, a_lambda_k1, a_lambda_q2, a_lambda_k2, a_out_norm, b_cq_norm, b_ckv_norm, b_w_uq, b_w_ukv, b_q_norm, b_k_norm, c_q_norm, c_k_norm, d_q_norm, d_k_norm, w_gate, w_branch, w_out, norm_ffn, w_ff1, w_ff2, norm_ple, w_ple_gate, w_ple_proj):
    depth = w_in.shape[0]
    d_model = x_prompt.shape[-1]
    groups = [(x_prompt.shape[0], x_prompt.shape[1]), (x_sample.shape[0], x_sample.shape[1])]
    x = jnp.concatenate([x_prompt.reshape(-1, d_model), x_sample.reshape(-1, d_model)], axis=0)
    m = x.shape[0]
    p_all = jnp.concatenate([p_prompt.reshape(depth, -1, p_prompt.shape[-1]),
                             p_sample.reshape(depth, -1, p_sample.shape[-1])], axis=1)
    row0s = [0, groups[0][0] * groups[0][1]]

    pos = jnp.concatenate([jnp.tile(jnp.arange(s, dtype=jnp.int32), b) for b, s in groups])
    posf = pos.astype(F32)
    cs_b = _rope_tables(posf, DR_B)
    cs_row = _rope_tables((pos // GRID_W).astype(F32), DH_C // 2)
    cs_col = _rope_tables((pos % GRID_W).astype(F32), DH_C // 2)

    in_sizes = (512, 512, 512, 512, 512, 64, 512, 256, 256, 1536, 512, 512)
    offs = np.concatenate([[0], np.cumsum(in_sizes)])
    seg = lambda w, a, b: w[:, offs[a]:offs[b]]
    n_in = 6656

    bias_a = rel_bias[:, :H_A]
    bias_d = rel_bias[:, H_A:]
    t_a = _pick(min(s for _, s in groups), (512, 256, 128))
    bias_tiles_a, rmax = _diff_bias_tiles(bias_a, t_a)
    tu = min(128, min(s for _, s in groups) // DILATIONS[-1][1])
    bias_tiles_d = _band_bias_tiles(bias_d, tu)

    for i in range(depth):
        lam_init = 0.8 - 0.6 * math.exp(-0.3 * i)
        wi = w_in[i]
        w_in_p = jnp.concatenate(
            [seg(wi, 0, 3), seg(wi, 6, 9), seg(wi, 9, 12), seg(wi, 3, 5), seg(wi, 5, 6)], axis=1)
        w_in_p = jnp.pad(w_in_p, ((0, 0), (0, n_in - w_in_p.shape[1]))).astype(BF16)

        h = rmsnorm(x, norm_mix[i])
        proj = matmul(h, w_in_p, out_dtype=F32)
        aq, ak, av = proj[:, 0:512], proj[:, 512:1024], proj[:, 1024:1536]
        cq, ck, cv = proj[:, 1536:2048], proj[:, 2048:2304], proj[:, 2304:2560]
        dq, dk, dv = proj[:, 2560:4096], proj[:, 4096:4608], proj[:, 4608:5120]
        bcq, bckv, bkr = proj[:, 5120:5632], proj[:, 5632:6144], proj[:, 6144:6208]

        qa = (_norm(aq.reshape(m, H_A, 2, DK_A), a_q_norm[i]) * DK_A ** -0.5).reshape(m, -1).astype(BF16)
        ka = _norm(ak.reshape(m, H_A, 2, DK_A), a_k_norm[i]).reshape(m, -1).astype(BF16)
        va = av.astype(BF16)
        lam = (jnp.exp(jnp.sum(a_lambda_q1[i].astype(F32) * a_lambda_k1[i].astype(F32)))
               - jnp.exp(jnp.sum(a_lambda_q2[i].astype(F32) * a_lambda_k2[i].astype(F32))) + lam_init)
        lam = lam.reshape(1).astype(F32)

        qb = matmul(rmsnorm(bcq, b_cq_norm[i]), b_w_uq[i].astype(BF16), out_dtype=F32)
        kvb = matmul(rmsnorm(bckv, b_ckv_norm[i]), b_w_ukv[i].astype(BF16), out_dtype=F32)
        qb = _norm(qb.reshape(m, H_B, DN_B + DR_B), b_q_norm[i])
        kvb = kvb.reshape(m, H_B, DN_B + DV_B)
        kb = jnp.concatenate([kvb[..., :DN_B], jnp.broadcast_to(bkr[:, None, :], (m, H_B, DR_B))], axis=-1)
        kb = _norm(kb, b_k_norm[i])
        vb = kvb[..., DN_B:].reshape(m, -1).astype(BF16)
        pad_b = ((0, 0), (0, 0), (0, DKP_B - DN_B - DR_B))
        qb = jnp.concatenate([qb[..., :DN_B], _rope(qb[..., DN_B:], cs_b)], axis=-1) * (DN_B + DR_B) ** -0.5
        kb = jnp.concatenate([kb[..., :DN_B], _rope(kb[..., DN_B:], cs_b)], axis=-1)
        qb = jnp.pad(qb, pad_b).reshape(m, -1).astype(BF16)
        kb = jnp.pad(kb, pad_b).reshape(m, -1).astype(BF16)

        def axial(t):
            hh = t.shape[-1] // 2
            return jnp.concatenate([_rope(t[..., :hh], cs_row), _rope(t[..., hh:], cs_col)], axis=-1)
        qc = (axial(_norm(cq.reshape(m, H_C, DH_C), c_q_norm[i])) * DH_C ** -0.5).reshape(m, -1).astype(BF16)
        kc = axial(_norm(ck.reshape(m, KV_C, DH_C), c_k_norm[i])).reshape(m, -1).astype(BF16)
        vc = cv.astype(BF16)

        qd = (_norm(dq.reshape(m, N_DIL, H_D, DH_D), d_q_norm[i]) * DH_D ** -0.5).reshape(m, -1).astype(BF16)
        kd = _norm(dk.reshape(m, H_D, DH_D), d_k_norm[i]).reshape(m, -1).astype(BF16)
        vd = dv.astype(BF16)

        br = [[], [], [], []]
        for (nseq, seq), row0 in zip(groups, row0s):
            kw = dict(row0=row0, nseq=nseq, seq=seq)
            br[0].append(diff_attention(qa, ka, va, bias_tiles_a, rmax, lam, a_out_norm[i],
                                        lam_init=lam_init, **kw))
            br[1].append(dense_attention(qb, kb, vb, hq=H_B, hk=H_B, dk=DKP_B, dv=DV_B, **kw))
            br[2].append(dense_attention(qc, kc, vc, hq=H_C, hk=KV_C, dk=DH_C, dv=DH_C, **kw))
            outs, lses = zip(*[band_attention(qd, kd, vd, bias_tiles_d, g, **kw) for g in range(N_DIL)])
            br[3].append(band_mixture(outs, lses))
        branches = [jnp.concatenate(b, axis=0) for b in br]

        merged = gated_merge(h, w_gate[i].astype(BF16), w_branch[i].astype(BF16), branches,
                             out_dtype=BF16)
        x = matmul(merged, w_out[i].astype(BF16), out_dtype=F32, res=x)
        h2 = rmsnorm(x, norm_ffn[i])
        u = matmul(h2, w_ff1[i].astype(BF16), out_dtype=BF16, act="relu2")
        x = matmul(u, w_ff2[i].astype(BF16), out_dtype=F32, res=x)
        h3 = rmsnorm(x, norm_ple[i])
        x = gated_merge(h3, w_ple_gate[i].astype(BF16)[None], w_ple_proj[i].astype(BF16)[None],
                        [p_all[i]], out_dtype=F32, res=x)

    y_prompt = x[:row0s[1]].reshape(x_prompt.shape)
    y_sample = x[row0s[1]:].reshape(x_sample.shape)
    return (y_prompt, y_sample)
```

```python
import functools
import math

import jax
import jax.numpy as jnp
import numpy as np
from jax import lax
from jax.experimental import pallas as pl
from jax.experimental.pallas import tpu as pltpu

BF16 = jnp.bfloat16
F32 = jnp.float32

GRID_W = 64
H_A, DK_A, DV_A = 4, 64, 128
H_B, DN_B, DR_B, DV_B = 4, 128, 64, 128
H_C, KV_C, DH_C = 4, 2, 128
H_D, DH_D = 4, 128
DILATIONS = ((128, 1), (512, 4), (2048, 16))
N_DIL = len(DILATIONS)
N_BUCKETS = 32
REL_MAX_DIST = 1024
ROPE_THETA = 10000.0
NORM_EPS = 1e-6
NEG_INF = -1e30
LOG2E = 1.4426950408889634
LANES = 128
DKP_B = 256
VMEM_LIMIT = 56 * 1024 * 1024


def _cparams(sem):
    return pltpu.CompilerParams(dimension_semantics=sem, vmem_limit_bytes=VMEM_LIMIT)


def _pick(n, prefs):
    for t in prefs:
        if n % t == 0:
            return t
    return n


def _rmsnorm_body(x_ref, g_ref, o_ref):
    x = x_ref[...]
    y = x * lax.rsqrt(jnp.mean(x * x, axis=-1, keepdims=True) + NORM_EPS)
    o_ref[...] = (y * g_ref[...]).astype(o_ref.dtype)


def rmsnorm(x, g, out_dtype=BF16):
    m, d = x.shape
    tm = _pick(m, (512, 256, 128, 64, 8))
    return pl.pallas_call(
        _rmsnorm_body,
        out_shape=jax.ShapeDtypeStruct((m, d), out_dtype),
        grid=(m // tm,),
        in_specs=[pl.BlockSpec((tm, d), lambda i: (i, 0)),
                  pl.BlockSpec((1, d), lambda i: (0, 0))],
        out_specs=pl.BlockSpec((tm, d), lambda i: (i, 0)),
        compiler_params=_cparams(("parallel",)),
        name="rmsnorm",
    )(x, g.reshape(1, d).astype(F32))


def _mm_body(*refs, act, has_res, nk):
    if has_res:
        a_ref, w_ref, r_ref, o_ref = refs[:4]
        scratch = refs[4:]
    else:
        a_ref, w_ref, o_ref = refs[:3]
        r_ref = None
        scratch = refs[3:]

    def finish(acc):
        if act == "relu2":
            acc = jnp.square(jnp.maximum(acc, 0.0))
        if has_res:
            acc = r_ref[...] + acc
        o_ref[...] = acc.astype(o_ref.dtype)

    part = jnp.dot(a_ref[...], w_ref[...], preferred_element_type=F32)
    if nk == 1:
        finish(part)
        return
    acc_ref, = scratch
    k = pl.program_id(2)

    @pl.when(k == 0)
    def _():
        acc_ref[...] = part

    @pl.when(k > 0)
    def _():
        acc_ref[...] += part

    @pl.when(k == nk - 1)
    def _():
        finish(acc_ref[...])


def matmul(a, w, *, out_dtype, act=None, res=None, tm=1024, tn=512, tk=2048):
    m, kdim = a.shape
    n = w.shape[1]
    tm = _pick(m, (tm, 512, 256, 128, 64, 8))
    tn = _pick(n, (tn, 256, 128))
    tk = _pick(kdim, (tk, 1024, 512))
    nk = kdim // tk
    in_specs = [pl.BlockSpec((tm, tk), lambda i, j, k: (i, k)),
                pl.BlockSpec((tk, tn), lambda i, j, k: (k, j))]
    args = [a, w]
    if res is not None:
        in_specs.append(pl.BlockSpec((tm, tn), lambda i, j, k: (i, j)))
        args.append(res)
    return pl.pallas_call(
        functools.partial(_mm_body, act=act, has_res=res is not None, nk=nk),
        out_shape=jax.ShapeDtypeStruct((m, n), out_dtype),
        grid=(m // tm, n // tn, nk),
        in_specs=in_specs,
        out_specs=pl.BlockSpec((tm, tn), lambda i, j, k: (i, j)),
        scratch_shapes=[pltpu.VMEM((tm, tn), F32)] if nk > 1 else [],
        compiler_params=_cparams(("parallel", "parallel", "arbitrary")),
        name="matmul",
    )(*args)


def _gated_body(*refs, nb, has_res):
    h_ref, wg_ref, wb_ref = refs[:3]
    br_refs = refs[3:3 + nb]
    rest = refs[3 + nb:]
    if has_res:
        r_ref, o_ref = rest
    else:
        o_ref, = rest
    h = h_ref[...]
    acc = None
    for b in range(nb):
        gate = jnp.dot(h, wg_ref[b], preferred_element_type=F32)
        val = jnp.dot(br_refs[b][...].astype(BF16), wb_ref[b], preferred_element_type=F32)
        term = jax.nn.sigmoid(gate) * val
        acc = term if acc is None else acc + term
    if has_res:
        acc = r_ref[...] + acc
    o_ref[...] = acc.astype(o_ref.dtype)


def gated_merge(h, wg, wb, branches, *, out_dtype, res=None, tm=1024, tn=256):
    m, d = h.shape
    nb, _, n = wg.shape
    kb = wb.shape[1]
    tm = _pick(m, (tm, 512, 256, 128, 64, 8))
    tn = _pick(n, (tn, 128))
    in_specs = [pl.BlockSpec((tm, d), lambda i, j: (i, 0)),
                pl.BlockSpec((nb, d, tn), lambda i, j: (0, 0, j)),
                pl.BlockSpec((nb, kb, tn), lambda i, j: (0, 0, j))]
    in_specs += [pl.BlockSpec((tm, kb), lambda i, j: (i, 0)) for _ in range(nb)]
    args = [h, wg, wb, *branches]
    if res is not None:
        in_specs.append(pl.BlockSpec((tm, tn), lambda i, j: (i, j)))
        args.append(res)
    return pl.pallas_call(
        functools.partial(_gated_body, nb=nb, has_res=res is not None),
        out_shape=jax.ShapeDtypeStruct((m, n), out_dtype),
        grid=(m // tm, n // tn),
        in_specs=in_specs,
        out_specs=pl.BlockSpec((tm, tn), lambda i, j: (i, j)),
        compiler_params=_cparams(("parallel", "parallel")),
        name="gated_merge",
    )(*args)


def _softmax_step_t(k, q, v, m_ref, l_ref, acc_ref, bias=None):
    st = _kq(k, q)
    if bias is not None:
        st = st + bias
    m_prev = m_ref[...]
    m_new = jnp.maximum(m_prev, jnp.max(st, axis=0, keepdims=True))
    alpha = jnp.exp2(m_prev - m_new)
    pt = jnp.exp2(st - m_new)
    l_ref[...] = alpha * l_ref[...] + jnp.sum(pt, axis=0, keepdims=True)
    pv = lax.dot_general(v, pt.astype(BF16), (((0,), (0,)), ((), ())), preferred_element_type=F32)
    acc_ref[...] = alpha * acc_ref[...] + pv
    m_ref[...] = m_new


def _init_stats(m_ref, l_ref, acc_ref):
    m_ref[...] = jnp.full(m_ref.shape, -jnp.inf, F32)
    l_ref[...] = jnp.zeros(l_ref.shape, F32)
    acc_ref[...] = jnp.zeros(acc_ref.shape, F32)


def _kq(k, q):
    return lax.dot_general(k, q, (((1,), (1,)), ((), ())), preferred_element_type=F32)


def _attn_body(q_ref, k_ref, v_ref, o_ref, m_ref, l_ref, acc_ref, *, tk, nk):
    _init_stats(m_ref, l_ref, acc_ref)
    q = q_ref[...]

    def step(j, carry):
        off = pl.multiple_of(j * tk, tk)
        _softmax_step_t(k_ref[pl.ds(off, tk), :], q, v_ref[pl.ds(off, tk), :], m_ref, l_ref, acc_ref)
        return carry

    lax.fori_loop(0, nk, step, 0)
    o_ref[...] = jnp.transpose(acc_ref[...] / l_ref[...]).astype(o_ref.dtype)


def _attn_scratch(tq, dv):
    return [pltpu.VMEM((1, tq), F32), pltpu.VMEM((1, tq), F32), pltpu.VMEM((dv, tq), F32)]


def dense_attention(q, k, v, *, row0, nseq, seq, hq, hk, dk, dv):
    tq = _pick(seq, (1024, 512, 256, 128))
    tk = _pick(seq, (512, 256, 128))
    nq = seq // tq
    group = hq // hk
    qb0 = row0 // tq
    sb0 = row0 // seq
    return pl.pallas_call(
        functools.partial(_attn_body, tk=tk, nk=seq // tk),
        out_shape=jax.ShapeDtypeStruct((nseq * seq, hq * dv), BF16),
        grid=(nseq, hq, nq),
        in_specs=[pl.BlockSpec((tq, dk), lambda b, h, i: (qb0 + b * nq + i, h)),
                  pl.BlockSpec((seq, dk), lambda b, h, i: (sb0 + b, h // group)),
                  pl.BlockSpec((seq, dv), lambda b, h, i: (sb0 + b, h // group))],
        out_specs=pl.BlockSpec((tq, dv), lambda b, h, i: (b * nq + i, h)),
        scratch_shapes=_attn_scratch(tq, dv),
        compiler_params=_cparams(("parallel", "parallel", "parallel")),
        name="dense_attention",
    )(q, k, v)


BIAS_FAR = 1152


def _diff_body(lam_ref, q_ref, k_ref, v_ref, bias_ref, g_ref, o_ref,
               m1, l1, a1, m2, l2, a2, *, tq, tk, nk, lo, out_scale):
    _init_stats(m1, l1, a1)
    _init_stats(m2, l2, a2)
    q = q_ref[...]
    lane = lax.broadcasted_iota(jnp.int32, q.shape, 1)
    zero = jnp.zeros_like(q)
    q1 = jnp.where(lane < DK_A, q, zero)
    q2 = jnp.where(lane >= DK_A, q, zero)
    q0 = pl.program_id(2) * tq

    def step(j, carry):
        off = pl.multiple_of(j * tk, tk)
        k = k_ref[pl.ds(off, tk), :]
        v = v_ref[pl.ds(off, tk), :]
        cols = []
        for c0 in range(0, tq, LANES):
            r0 = jnp.clip(off - q0 - c0, lo, BIAS_FAR) - lo
            cols.append(bias_ref[pl.ds(pl.multiple_of(r0, LANES), tk), :])
        bias = jnp.concatenate(cols, axis=1)
        _softmax_step_t(k, q1, v, m1, l1, a1, bias)
        _softmax_step_t(k, q2, v, m2, l2, a2, bias)
        return carry

    lax.fori_loop(0, nk, step, 0)
    o = a1[...] / l1[...] - lam_ref[0] * (a2[...] / l2[...])
    y = o * lax.rsqrt(jnp.mean(o * o, axis=0, keepdims=True) + NORM_EPS)
    o_ref[...] = jnp.transpose((y * g_ref[...]) * out_scale).astype(o_ref.dtype)


def _rel_bucket(rel):
    half = N_BUCKETS // 2
    exact = half // 2
    n = jnp.abs(rel)
    nf = jnp.maximum(n, 1).astype(F32)
    large = exact + (jnp.log(nf / exact) / math.log(REL_MAX_DIST / exact) * (half - exact)).astype(jnp.int32)
    large = jnp.minimum(large, half - 1)
    return jnp.where(rel > 0, half, 0) + jnp.where(n < exact, n, large)


def _diagonal_table(tbl, lo, nrows, dil=1):
    rel = jnp.arange(lo - (LANES - 1), lo + nrows, dtype=jnp.int32)
    vec = jnp.transpose(tbl[_rel_bucket(rel * dil)]).astype(F32)
    return jnp.stack([vec[:, LANES - 1 - l:LANES - 1 - l + nrows] for l in range(LANES)], axis=-1)


def diff_attention(q, k, v, bias_a, lam, out_g, *, row0, nseq, seq, lam_init):
    tq = _pick(seq, (1024, 512, 256, 128))
    tk = _pick(seq, (512, 256, 128))
    nq = seq // tq
    qb0 = row0 // tq
    sb0 = row0 // seq
    lo = -(REL_MAX_DIST + tk)
    nrows = BIAS_FAR - lo + tk
    table = _diagonal_table(bias_a * LOG2E, lo, nrows)
    return pl.pallas_call(
        functools.partial(_diff_body, tq=tq, tk=tk, nk=seq // tk, lo=lo, out_scale=1.0 - lam_init),
        out_shape=jax.ShapeDtypeStruct((nseq * seq, H_A * DV_A), BF16),
        grid=(nseq, H_A, nq),
        in_specs=[pl.BlockSpec(memory_space=pltpu.SMEM),
                  pl.BlockSpec((tq, 2 * DK_A), lambda b, h, i: (qb0 + b * nq + i, h)),
                  pl.BlockSpec((seq, 2 * DK_A), lambda b, h, i: (sb0 + b, h)),
                  pl.BlockSpec((seq, DV_A), lambda b, h, i: (sb0 + b, h)),
                  pl.BlockSpec((None, nrows, LANES), lambda b, h, i: (h, 0, 0)),
                  pl.BlockSpec((DV_A, 1), lambda b, h, i: (0, 0))],
        out_specs=pl.BlockSpec((tq, DV_A), lambda b, h, i: (b * nq + i, h)),
        scratch_shapes=_attn_scratch(tq, DV_A) + _attn_scratch(tq, DV_A),
        compiler_params=_cparams(("parallel", "parallel", "parallel")),
        name="diff_attention",
    )(lam, q, k, v, table, out_g.reshape(DV_A, 1).astype(F32))


def _band_body(q_ref, kp_ref, kc_ref, kn_ref, vp_ref, vc_ref, vn_ref, bias_ref, o_ref, lse_ref,
               *, tu, length):
    n = pl.program_id(3)
    k = jnp.concatenate([kp_ref[...], kc_ref[...], kn_ref[...]], axis=0)
    v = jnp.concatenate([vp_ref[...], vc_ref[...], vn_ref[...]], axis=0)
    s = _kq(q_ref[...], k) + bias_ref[...]
    ku = (n - 1) * tu + lax.broadcasted_iota(jnp.int32, s.shape, 1)
    s = jnp.where((ku >= 0) & (ku < length), s, NEG_INF)
    m = jnp.max(s, axis=1, keepdims=True)
    p = jnp.exp(s - m)
    l = jnp.sum(p, axis=1, keepdims=True)
    o = jnp.dot(p.astype(BF16), v, preferred_element_type=F32) / l
    o_ref[...] = o.astype(o_ref.dtype)
    lse_ref[...] = jnp.broadcast_to(m + jnp.log(l), lse_ref.shape)


def _band_bias_tiles(bias_d, tu):
    tiles = []
    qi = jnp.arange(tu, dtype=jnp.int32)[:, None]
    kj = jnp.arange(3 * tu, dtype=jnp.int32)[None, :]
    rel = kj - tu - qi
    for g, (window, dil) in enumerate(DILATIONS):
        half = window // (2 * dil)
        tbl = bias_d[:, g * H_D:(g + 1) * H_D]
        bias = jnp.transpose(tbl[_rel_bucket(rel * dil)], (2, 0, 1))
        tiles.append(jnp.where((jnp.abs(rel) <= half)[None], bias, NEG_INF))
    return jnp.stack(tiles).astype(F32)


def band_attention(q, k, v, bias_tiles, g, *, row0, nseq, seq):
    _, dil = DILATIONS[g]
    length = seq // dil
    tu = bias_tiles.shape[2]
    nb = length // tu
    m = q.shape[0]
    qv = q.reshape(m // dil, dil * N_DIL * H_D * DH_D)
    kv = k.reshape(m // dil, dil * H_D * DH_D)
    vv = v.reshape(m // dil, dil * H_D * DH_D)
    ub0 = row0 // dil // tu

    def rows(b, n):
        return ub0 + b * nb + n

    def kspec(shift):
        return pl.BlockSpec(
            (tu, DH_D),
            lambda b, r, h, n: (rows(b, jnp.clip(n + shift, 0, nb - 1)), r * H_D + h))

    mo = nseq * seq
    out_spec = pl.BlockSpec((tu, DH_D), lambda b, r, h, n: (b * nb + n, r * H_D + h))
    o, lse = pl.pallas_call(
        functools.partial(_band_body, tu=tu, length=length),
        out_shape=(jax.ShapeDtypeStruct((mo // dil, dil * H_D * DH_D), F32),
                   jax.ShapeDtypeStruct((mo // dil, dil * H_D * DH_D), F32)),
        grid=(nseq, dil, H_D, nb),
        in_specs=[pl.BlockSpec((tu, DH_D), lambda b, r, h, n: (rows(b, n), (r * N_DIL + g) * H_D + h)),
                  kspec(-1), kspec(0), kspec(1), kspec(-1), kspec(0), kspec(1),
                  pl.BlockSpec((None, None, tu, 3 * tu), lambda b, r, h, n: (g, h, 0, 0))],
        out_specs=(out_spec, out_spec),
        compiler_params=_cparams(("parallel", "parallel", "parallel", "parallel")),
        name="band_attention",
    )(qv, kv, kv, kv, vv, vv, vv, bias_tiles)
    return o.reshape(mo, H_D * DH_D), lse.reshape(mo, H_D * DH_D)


def _mix_body(*refs):
    o_refs = refs[:N_DIL]
    l_refs = refs[N_DIL:2 * N_DIL]
    out_ref = refs[2 * N_DIL]
    ls = [r[...] for r in l_refs]
    mx = functools.reduce(jnp.maximum, ls)
    es = [jnp.exp(l - mx) for l in ls]
    den = functools.reduce(lambda a, b: a + b, es)
    acc = None
    for e, o_ref in zip(es, o_refs):
        term = (e / den) * o_ref[...]
        acc = term if acc is None else acc + term
    out_ref[...] = acc.astype(out_ref.dtype)


def band_mixture(outs, lses):
    m, c = outs[0].shape
    tm = _pick(m, (1024, 512, 256, 128, 64, 8))
    spec = pl.BlockSpec((tm, c), lambda i: (i, 0))
    return pl.pallas_call(
        _mix_body,
        out_shape=jax.ShapeDtypeStruct((m, c), BF16),
        grid=(m // tm,),
        in_specs=[spec] * (2 * N_DIL),
        out_specs=spec,
        compiler_params=_cparams(("parallel",)),
        name="band_mixture",
    )(*outs, *lses)


def _norm(x, g):
    return x * lax.rsqrt(jnp.mean(x * x, axis=-1, keepdims=True) + NORM_EPS) * g


def _rope_tables(pos, d):
    half = d // 2
    inv = jnp.power(ROPE_THETA, -2.0 * jnp.arange(half, dtype=F32) / d)
    ang = pos[:, None] * inv[None, :]
    return jnp.cos(ang)[:, None, :], jnp.sin(ang)[:, None, :]


def _rope(x, cs):
    cos, sin = cs
    half = x.shape[-1] // 2
    x1, x2 = x[..., :half], x[..., half:]
    return jnp.concatenate([x1 * cos - x2 * sin, x2 * cos + x1 * sin], axis=-1)


def kernel(x_prompt, x_sample, p_prompt, p_sample, rel_bias, norm_mix, w_in, a_q_norm, a_k_norm, a_lambda_q1, a_lambda_k1, a_lambda_q2, a_lambda_k2, a_out_norm, b_cq_norm, b_ckv_norm, b_w_uq, b_w_ukv, b_q_norm, b_k_norm, c_q_norm, c_k_norm, d_q_norm, d_k_norm, w_gate, w_branch, w_out, norm_ffn, w_ff1, w_ff2, norm_ple, w_ple_gate, w_ple_proj):
    depth = w_in.shape[0]
    d_model = x_prompt.shape[-1]
    groups = [(x_prompt.shape[0], x_prompt.shape[1]), (x_sample.shape[0], x_sample.shape[1])]
    x = jnp.concatenate([x_prompt.reshape(-1, d_model), x_sample.reshape(-1, d_model)], axis=0)
    m = x.shape[0]
    p_all = jnp.concatenate([p_prompt.reshape(depth, -1, p_prompt.shape[-1]),
                             p_sample.reshape(depth, -1, p_sample.shape[-1])], axis=1)
    row0s = [0, groups[0][0] * groups[0][1]]

    pos = jnp.concatenate([jnp.tile(jnp.arange(s, dtype=jnp.int32), b) for b, s in groups])
    posf = pos.astype(F32)
    cs_b = _rope_tables(posf, DR_B)
    cs_row = _rope_tables((pos // GRID_W).astype(F32), DH_C // 2)
    cs_col = _rope_tables((pos % GRID_W).astype(F32), DH_C // 2)

    in_sizes = (512, 512, 512, 512, 512, 64, 512, 256, 256, 1536, 512, 512)
    offs = np.concatenate([[0], np.cumsum(in_sizes)])
    seg = lambda w, a, b: w[:, offs[a]:offs[b]]
    n_in = 6656

    bias_a = rel_bias[:, :H_A]
    bias_d = rel_bias[:, H_A:]
    tu = min(128, min(s for _, s in groups) // DILATIONS[-1][1])
    bias_tiles_d = _band_bias_tiles(bias_d, tu)

    for i in range(depth):
        lam_init = 0.8 - 0.6 * math.exp(-0.3 * i)
        wi = w_in[i]
        w_in_p = jnp.concatenate(
            [seg(wi, 0, 3), seg(wi, 6, 9), seg(wi, 9, 12), seg(wi, 3, 5), seg(wi, 5, 6)], axis=1)
        w_in_p = jnp.pad(w_in_p, ((0, 0), (0, n_in - w_in_p.shape[1]))).astype(BF16)

        h = rmsnorm(x, norm_mix[i])
        proj = matmul(h, w_in_p, out_dtype=F32)
        aq, ak, av = proj[:, 0:512], proj[:, 512:1024], proj[:, 1024:1536]
        cq, ck, cv = proj[:, 1536:2048], proj[:, 2048:2304], proj[:, 2304:2560]
        dq, dk, dv = proj[:, 2560:4096], proj[:, 4096:4608], proj[:, 4608:5120]
        bcq, bckv, bkr = proj[:, 5120:5632], proj[:, 5632:6144], proj[:, 6144:6208]

        qa = (_norm(aq.reshape(m, H_A, 2, DK_A), a_q_norm[i]) * (DK_A ** -0.5 * LOG2E)).reshape(m, -1).astype(BF16)
        ka = _norm(ak.reshape(m, H_A, 2, DK_A), a_k_norm[i]).reshape(m, -1).astype(BF16)
        va = av.astype(BF16)
        lam = (jnp.exp(jnp.sum(a_lambda_q1[i].astype(F32) * a_lambda_k1[i].astype(F32)))
               - jnp.exp(jnp.sum(a_lambda_q2[i].astype(F32) * a_lambda_k2[i].astype(F32))) + lam_init)
        lam = lam.reshape(1).astype(F32)

        qb = matmul(rmsnorm(bcq, b_cq_norm[i]), b_w_uq[i].astype(BF16), out_dtype=F32)
        kvb = matmul(rmsnorm(bckv, b_ckv_norm[i]), b_w_ukv[i].astype(BF16), out_dtype=F32)
        qb = _norm(qb.reshape(m, H_B, DN_B + DR_B), b_q_norm[i])
        kvb = kvb.reshape(m, H_B, DN_B + DV_B)
        kb = jnp.concatenate([kvb[..., :DN_B], jnp.broadcast_to(bkr[:, None, :], (m, H_B, DR_B))], axis=-1)
        kb = _norm(kb, b_k_norm[i])
        vb = kvb[..., DN_B:].reshape(m, -1).astype(BF16)
        pad_b = ((0, 0), (0, 0), (0, DKP_B - DN_B - DR_B))
        qb = jnp.concatenate([qb[..., :DN_B], _rope(qb[..., DN_B:], cs_b)], axis=-1) * ((DN_B + DR_B) ** -0.5 * LOG2E)
        kb = jnp.concatenate([kb[..., :DN_B], _rope(kb[..., DN_B:], cs_b)], axis=-1)
        qb = jnp.pad(qb, pad_b).reshape(m, -1).astype(BF16)
        kb = jnp.pad(kb, pad_b).reshape(m, -1).astype(BF16)

        def axial(t):
            hh = t.shape[-1] // 2
            return jnp.concatenate([_rope(t[..., :hh], cs_row), _rope(t[..., hh:], cs_col)], axis=-1)
        qc = (axial(_norm(cq.reshape(m, H_C, DH_C), c_q_norm[i])) * (DH_C ** -0.5 * LOG2E)).reshape(m, -1).astype(BF16)
        kc = axial(_norm(ck.reshape(m, KV_C, DH_C), c_k_norm[i])).reshape(m, -1).astype(BF16)
        vc = cv.astype(BF16)

        qd = (_norm(dq.reshape(m, N_DIL, H_D, DH_D), d_q_norm[i]) * DH_D ** -0.5).reshape(m, -1).astype(BF16)
        kd = _norm(dk.reshape(m, H_D, DH_D), d_k_norm[i]).reshape(m, -1).astype(BF16)
        vd = dv.astype(BF16)

        br = [[], [], [], []]
        for (nseq, seq), row0 in zip(groups, row0s):
            kw = dict(row0=row0, nseq=nseq, seq=seq)
            br[0].append(diff_attention(qa, ka, va, bias_a, lam, a_out_norm[i], lam_init=lam_init, **kw))
            br[1].append(dense_attention(qb, kb, vb, hq=H_B, hk=H_B, dk=DKP_B, dv=DV_B, **kw))
            br[2].append(dense_attention(qc, kc, vc, hq=H_C, hk=KV_C, dk=DH_C, dv=DH_C, **kw))
            outs, lses = zip(*[band_attention(qd, kd, vd, bias_tiles_d, g, **kw) for g in range(N_DIL)])
            br[3].append(band_mixture(outs, lses))
        branches = [jnp.concatenate(b, axis=0) for b in br]

        merged = gated_merge(h, w_gate[i].astype(BF16), w_branch[i].astype(BF16), branches,
                             out_dtype=BF16)
        x = matmul(merged, w_out[i].astype(BF16), out_dtype=F32, res=x)
        h2 = rmsnorm(x, norm_ffn[i])
        u = matmul(h2, w_ff1[i].astype(BF16), out_dtype=BF16, act="relu2")
        x = matmul(u, w_ff2[i].astype(BF16), out_dtype=F32, res=x)
        h3 = rmsnorm(x, norm_ple[i])
        x = gated_merge(h3, w_ple_gate[i].astype(BF16)[None], w_ple_proj[i].astype(BF16)[None],
                        [p_all[i]], out_dtype=F32, res=x)

    y_prompt = x[:row0s[1]].reshape(x_prompt.shape)
    y_sample = x[row0s[1]:].reshape(x_sample.shape)
    return (y_prompt, y_sample)
```

```python
import functools
import math

import jax
import jax.numpy as jnp
import numpy as np
from jax import lax
from jax.experimental import pallas as pl
from jax.experimental.pallas import tpu as pltpu

BF16 = jnp.bfloat16
F32 = jnp.float32

GRID_W = 64
H_A, DK_A, DV_A = 4, 64, 128
H_B, DN_B, DR_B, DV_B = 4, 128, 64, 128
Q_RANK_B = KV_RANK_B = 512
H_C, KV_C, DH_C = 4, 2, 128
H_D, DH_D = 4, 128
DILATIONS = ((128, 1), (512, 4), (2048, 16))
N_DIL = len(DILATIONS)
HALO = 64
assert all(w // (2 * d) == HALO for w, d in DILATIONS)
N_BUCKETS = 32
REL_MAX_DIST = 1024
ROPE_THETA = 10000.0
NORM_EPS = 1e-6
NEG_INF = -1e30
LOG2E = 1.4426950408889634
LANES = 128
DKP_B = 256
VMEM_LIMIT = 56 * 1024 * 1024

MM_TM, MM_TN, MM_TK = 1024, 512, 2048
GATED_TN = 256
ATTN_TQ, ATTN_TK = 2048, 512
PREP_TM = 256
BAND_TU = 256

COL_A, COL_C, COL_D, COL_BCQ, COL_BCKV, COL_KR, N_IN = 0, 1536, 2560, 5120, 5632, 6144, 6656


def _cparams(sem):
    return pltpu.CompilerParams(dimension_semantics=sem, vmem_limit_bytes=VMEM_LIMIT)


def _pick(n, prefs):
    for t in prefs:
        if n % t == 0:
            return t
    return n


def _rmsnorm_body(x_ref, g_ref, o_ref):
    x = x_ref[...]
    y = x * lax.rsqrt(jnp.mean(x * x, axis=-1, keepdims=True) + NORM_EPS)
    o_ref[...] = (y * g_ref[...]).astype(o_ref.dtype)


def rmsnorm(x, g, out_dtype=BF16):
    m, d = x.shape
    tm = _pick(m, (512, 256, 128, 64, 8))
    return pl.pallas_call(
        _rmsnorm_body,
        out_shape=jax.ShapeDtypeStruct((m, d), out_dtype),
        grid=(m // tm,),
        in_specs=[pl.BlockSpec((tm, d), lambda i: (i, 0)),
                  pl.BlockSpec((1, d), lambda i: (0, 0))],
        out_specs=pl.BlockSpec((tm, d), lambda i: (i, 0)),
        compiler_params=_cparams(("parallel",)),
        name="rmsnorm",
    )(x, g.reshape(1, d).astype(F32))


def _mm_body(*refs, act, has_res, nk):
    if has_res:
        a_ref, w_ref, r_ref, o_ref = refs[:4]
        scratch = refs[4:]
    else:
        a_ref, w_ref, o_ref = refs[:3]
        r_ref = None
        scratch = refs[3:]

    def finish(acc):
        if act == "relu2":
            acc = jnp.square(jnp.maximum(acc, 0.0))
        if has_res:
            acc = r_ref[...] + acc
        o_ref[...] = acc.astype(o_ref.dtype)

    part = jnp.dot(a_ref[...], w_ref[...], preferred_element_type=F32)
    if nk == 1:
        finish(part)
        return
    acc_ref, = scratch
    k = pl.program_id(2)

    @pl.when(k == 0)
    def _():
        acc_ref[...] = part

    @pl.when(k > 0)
    def _():
        acc_ref[...] += part

    @pl.when(k == nk - 1)
    def _():
        finish(acc_ref[...])


def matmul(a, w, *, out_dtype, act=None, res=None):
    m, kdim = a.shape
    n = w.shape[1]
    tm = _pick(m, (MM_TM, 512, 256, 128, 64, 8))
    tn = _pick(n, (MM_TN, 256, 128))
    tk = _pick(kdim, (MM_TK, 1024, 512))
    nk = kdim // tk
    in_specs = [pl.BlockSpec((tm, tk), lambda i, j, k: (i, k)),
                pl.BlockSpec((tk, tn), lambda i, j, k: (k, j))]
    args = [a, w]
    if res is not None:
        in_specs.append(pl.BlockSpec((tm, tn), lambda i, j, k: (i, j)))
        args.append(res)
    return pl.pallas_call(
        functools.partial(_mm_body, act=act, has_res=res is not None, nk=nk),
        out_shape=jax.ShapeDtypeStruct((m, n), out_dtype),
        grid=(m // tm, n // tn, nk),
        in_specs=in_specs,
        out_specs=pl.BlockSpec((tm, tn), lambda i, j, k: (i, j)),
        scratch_shapes=[pltpu.VMEM((tm, tn), F32)] if nk > 1 else [],
        compiler_params=_cparams(("parallel", "parallel", "arbitrary")),
        name="matmul",
    )(*args)


def _gated_body(*refs, nb, npart, split, has_res):
    h_ref, wg_ref, wb_ref = refs[:3]
    br_refs = refs[3:3 + nb * npart]
    rest = refs[3 + nb * npart:]
    if has_res:
        r_ref, o_ref = rest
    else:
        o_ref, = rest
    h = h_ref[...]
    first = pl.program_id(0) < split
    acc = None
    for b in range(nb):
        parts = br_refs[b * npart:(b + 1) * npart]
        x = parts[0][...]
        if npart == 2:
            x = jnp.where(first, x, parts[1][...])
        gate = jnp.dot(h, wg_ref[b], preferred_element_type=F32)
        val = jnp.dot(x.astype(BF16), wb_ref[b], preferred_element_type=F32)
        term = jax.nn.sigmoid(gate) * val
        acc = term if acc is None else acc + term
    if has_res:
        acc = r_ref[...] + acc
    o_ref[...] = acc.astype(o_ref.dtype)


def gated_merge(h, wg, wb, branches, *, out_dtype, res=None):
    m, d = h.shape
    nb, _, n = wg.shape
    kb = wb.shape[1]
    tm = _pick(m, (MM_TM, 512, 256, 128, 64, 8))
    tn = _pick(n, (GATED_TN, 128))
    npart = len(branches[0])
    rows0 = branches[0][0].shape[0]
    split = rows0 // tm if npart == 2 else m // tm
    assert npart in (1, 2) and rows0 % tm == 0
    in_specs = [pl.BlockSpec((tm, d), lambda i, j: (i, 0)),
                pl.BlockSpec((nb, d, tn), lambda i, j: (0, 0, j)),
                pl.BlockSpec((nb, kb, tn), lambda i, j: (0, 0, j))]
    args = [h, wg, wb]
    for parts in branches:
        in_specs.append(pl.BlockSpec((tm, kb), lambda i, j: (jnp.minimum(i, split - 1), 0)))
        if npart == 2:
            in_specs.append(pl.BlockSpec((tm, kb), lambda i, j: (jnp.maximum(i - split, 0), 0)))
        args.extend(parts)
    if res is not None:
        in_specs.append(pl.BlockSpec((tm, tn), lambda i, j: (i, j)))
        args.append(res)
    return pl.pallas_call(
        functools.partial(_gated_body, nb=nb, npart=npart, split=split, has_res=res is not None),
        out_shape=jax.ShapeDtypeStruct((m, n), out_dtype),
        grid=(m // tm, n // tn),
        in_specs=in_specs,
        out_specs=pl.BlockSpec((tm, tn), lambda i, j: (i, j)),
        compiler_params=_cparams(("parallel", "parallel")),
        name="gated_merge",
    )(*args)


def _rope_rot(y, cos, sin_signed, first):
    partner = jnp.where(first, pltpu.roll(y, 96, 1), pltpu.roll(y, 32, 1))
    return y * cos + partner * sin_signed


def _inv_rms(ss, n):
    return lax.rsqrt(ss * (1.0 / n) + NORM_EPS)


def _rowsum(x):
    return jnp.sum(x, axis=1, keepdims=True)


def _prep_body(proj_ref, cosb_ref, sinb_ref, cosc_ref, sinc_ref, g128_ref, g512_ref, wuq_ref, wukv_ref,
               qa_ref, ka_ref, va_ref, qb_ref, kb_ref, vb_ref, qc_ref, kc_ref, vc_ref,
               qd_ref, kd_ref, vd_ref):
    lane = lax.broadcasted_iota(jnp.int32, (1, LANES), 1)
    low = lane < DK_A
    first = (lane % 64) < 32
    blk = lambda c0, j: slice(c0 + j * LANES, c0 + (j + 1) * LANES)
    gain = lambda r: g128_ref[r:r + 1, :]

    for base, grow, dst in ((COL_A, 0, qa_ref), (COL_A + 512, 1, ka_ref)):
        for h in range(H_A):
            x = proj_ref[:, blk(base, h)]
            x2 = x * x
            r = jnp.where(low, _inv_rms(_rowsum(jnp.where(low, x2, 0.0)), DK_A),
                          _inv_rms(_rowsum(jnp.where(low, 0.0, x2)), DK_A))
            dst[:, blk(0, h)] = (x * r * gain(grow)).astype(BF16)
    va_ref[...] = proj_ref[:, COL_A + 1024:COL_A + 1536].astype(BF16)

    cosc, sinc = cosc_ref[...], sinc_ref[...]
    for base, nh, grow, dst in ((COL_C, H_C, 2, qc_ref), (COL_C + 512, KV_C, 3, kc_ref)):
        for h in range(nh):
            x = proj_ref[:, blk(base, h)]
            y = x * _inv_rms(_rowsum(x * x), DH_C) * gain(grow)
            dst[:, blk(0, h)] = _rope_rot(y, cosc, sinc, first).astype(BF16)
    vc_ref[...] = proj_ref[:, COL_C + 768:COL_C + 1024].astype(BF16)

    for base, nh, grow, dst in ((COL_D, N_DIL * H_D, 4, qd_ref), (COL_D + 1536, H_D, 5, kd_ref)):
        for h in range(nh):
            x = proj_ref[:, blk(base, h)]
            dst[:, blk(0, h)] = (x * _inv_rms(_rowsum(x * x), DH_D) * gain(grow)).astype(BF16)
    vd_ref[...] = proj_ref[:, COL_D + 2048:COL_D + 2560].astype(BF16)

    cosb, sinb = cosb_ref[...], sinb_ref[...]
    cq = proj_ref[:, COL_BCQ:COL_BCQ + Q_RANK_B]
    cq = (cq * _inv_rms(_rowsum(cq * cq), Q_RANK_B) * g512_ref[0:1, :]).astype(BF16)
    ckv = proj_ref[:, COL_BCKV:COL_BCKV + KV_RANK_B]
    ckv = (ckv * _inv_rms(_rowsum(ckv * ckv), KV_RANK_B) * g512_ref[1:2, :]).astype(BF16)
    qraw = jnp.dot(cq, wuq_ref[...], preferred_element_type=F32)
    kvraw = jnp.dot(ckv, wukv_ref[...], preferred_element_type=F32)
    kr = proj_ref[:, COL_KR:COL_KR + LANES]
    kr_ss = _rowsum(kr * kr)
    gq_n, gq_r = g512_ref[2:3, 0:128], g512_ref[2:3, 128:256]
    gk_n, gk_r = g512_ref[2:3, 256:384], g512_ref[2:3, 384:512]
    for h in range(H_B):
        xn, xr = qraw[:, blk(0, 2 * h)], qraw[:, blk(0, 2 * h + 1)]
        r = _inv_rms(_rowsum(xn * xn) + _rowsum(xr * xr), DN_B + DR_B)
        qb_ref[:, blk(0, 2 * h)] = (xn * r * gq_n).astype(BF16)
        qb_ref[:, blk(0, 2 * h + 1)] = _rope_rot(xr * r * gq_r, cosb, sinb, first).astype(BF16)
        kn = kvraw[:, blk(0, 2 * h)]
        r = _inv_rms(_rowsum(kn * kn) + kr_ss, DN_B + DR_B)
        kb_ref[:, blk(0, 2 * h)] = (kn * r * gk_n).astype(BF16)
        kb_ref[:, blk(0, 2 * h + 1)] = _rope_rot(kr * r * gk_r, cosb, sinb, first).astype(BF16)
        vb_ref[:, blk(0, h)] = kvraw[:, blk(0, 2 * h + 1)].astype(BF16)


def _rope_lanes(pos):
    half = DR_B // 2
    inv = jnp.power(ROPE_THETA, -2.0 * jnp.arange(half, dtype=F32) / DR_B)
    ang = pos[:, None] * inv[None, :]
    cos, sin = jnp.cos(ang), jnp.sin(ang)
    return jnp.concatenate([cos, cos], axis=1), jnp.concatenate([-sin, sin], axis=1)


def prepare_mixer_inputs(proj, groups, gains128, gains512, w_uq_p, w_ukv):
    m = proj.shape[0]
    tm = _pick(math.gcd(*[s for _, s in groups]), (PREP_TM, 128, 64, 8))
    smax = max(s for _, s in groups)
    pos = jnp.arange(smax, dtype=jnp.int32)
    cb, sb = _rope_lanes(pos.astype(F32))
    cosb = jnp.concatenate([cb, jnp.ones((smax, 64), F32)], axis=1)
    sinb = jnp.concatenate([sb, jnp.zeros((smax, 64), F32)], axis=1)
    cr, sr = _rope_lanes((pos // GRID_W).astype(F32))
    cc, sc = _rope_lanes((pos % GRID_W).astype(F32))
    cosc = jnp.concatenate([cr, cc], axis=1)
    sinc = jnp.concatenate([sr, sc], axis=1)
    (b0, s0), (_, s1) = groups
    t0, q1 = b0 * s0 // tm, s1 // tm

    def pos_block(i):
        return (jnp.where(i < t0, i % (s0 // tm), (i - t0) % q1), 0)

    row = lambda c: pl.BlockSpec((tm, c), lambda i: (i, 0))
    full = lambda a: pl.BlockSpec(a.shape, lambda i: (0,) * a.ndim)
    tab = pl.BlockSpec((tm, LANES), pos_block)
    widths = (512, 512, 512, H_B * DKP_B, H_B * DKP_B, H_B * DV_B, 512, 256, 256,
              N_DIL * H_D * DH_D, H_D * DH_D, H_D * DH_D)
    return pl.pallas_call(
        _prep_body,
        out_shape=tuple(jax.ShapeDtypeStruct((m, c), BF16) for c in widths),
        grid=(m // tm,),
        in_specs=[row(N_IN), tab, tab, tab, tab, full(gains128), full(gains512), full(w_uq_p), full(w_ukv)],
        out_specs=tuple(row(c) for c in widths),
        compiler_params=_cparams(("parallel",)),
        name="prepare_mixer_inputs",
    )(proj, cosb, sinb, cosc, sinc, gains128, gains512, w_uq_p, w_ukv)


def _softmax_step_t(k, q, v, m_ref, l_ref, acc_ref, bias=None):
    st = _kq(k, q)
    if bias is not None:
        st = st + bias
    m_prev = m_ref[...]
    m_new = jnp.maximum(m_prev, jnp.max(st, axis=0, keepdims=True))
    alpha = jnp.exp2(m_prev - m_new)
    pt = jnp.exp2(st - m_new)
    l_ref[...] = alpha * l_ref[...] + jnp.sum(pt, axis=0, keepdims=True)
    pv = lax.dot_general(v, pt.astype(BF16), (((0,), (0,)), ((), ())), preferred_element_type=F32)
    acc_ref[...] = alpha * acc_ref[...] + pv
    m_ref[...] = m_new


def _init_stats(m_ref, l_ref, acc_ref):
    m_ref[...] = jnp.full(m_ref.shape, -jnp.inf, F32)
    l_ref[...] = jnp.zeros(l_ref.shape, F32)
    acc_ref[...] = jnp.zeros(acc_ref.shape, F32)


def _kq(k, q):
    return lax.dot_general(k, q, (((1,), (1,)), ((), ())), preferred_element_type=F32)


def _attn_body(q_ref, k_ref, v_ref, o_ref, m_ref, l_ref, acc_ref, *, tk, nk):
    _init_stats(m_ref, l_ref, acc_ref)
    q = q_ref[...]

    def step(j, carry):
        off = pl.multiple_of(j * tk, tk)
        _softmax_step_t(k_ref[pl.ds(off, tk), :], q, v_ref[pl.ds(off, tk), :], m_ref, l_ref, acc_ref)
        return carry

    lax.fori_loop(0, nk, step, 0)
    o_ref[...] = jnp.transpose(acc_ref[...] / l_ref[...]).astype(o_ref.dtype)


def _attn_scratch(tq, dv):
    return [pltpu.VMEM((1, tq), F32), pltpu.VMEM((1, tq), F32), pltpu.VMEM((dv, tq), F32)]


def _attn_tiles(seq):
    return _pick(seq, (ATTN_TQ, 1024, 512, 256, 128)), _pick(seq, (ATTN_TK, 256, 128))


def dense_attention(q, k, v, *, row0, nseq, seq, hq, hk, dk, dv):
    tq, tk = _attn_tiles(seq)
    nq = seq // tq
    group = hq // hk
    qb0 = row0 // tq
    sb0 = row0 // seq
    return pl.pallas_call(
        functools.partial(_attn_body, tk=tk, nk=seq // tk),
        out_shape=jax.ShapeDtypeStruct((nseq * seq, hq * dv), BF16),
        grid=(nseq, hq, nq),
        in_specs=[pl.BlockSpec((tq, dk), lambda b, h, i: (qb0 + b * nq + i, h)),
                  pl.BlockSpec((seq, dk), lambda b, h, i: (sb0 + b, h // group)),
                  pl.BlockSpec((seq, dv), lambda b, h, i: (sb0 + b, h // group))],
        out_specs=pl.BlockSpec((tq, dv), lambda b, h, i: (b * nq + i, h)),
        scratch_shapes=_attn_scratch(tq, dv),
        compiler_params=_cparams(("parallel", "parallel", "parallel")),
        name="dense_attention",
    )(q, k, v)


BIAS_FAR = 1152


def _diff_body(lam_ref, q_ref, k_ref, v_ref, bias_ref, g_ref, o_ref,
               m1, l1, a1, m2, l2, a2, *, tq, tk, nk, lo, out_scale):
    _init_stats(m1, l1, a1)
    _init_stats(m2, l2, a2)
    q = q_ref[...]
    lane = lax.broadcasted_iota(jnp.int32, q.shape, 1)
    zero = jnp.zeros_like(q)
    q1 = jnp.where(lane < DK_A, q, zero)
    q2 = jnp.where(lane >= DK_A, q, zero)
    q0 = pl.program_id(2) * tq

    def step(j, carry):
        off = pl.multiple_of(j * tk, tk)
        k = k_ref[pl.ds(off, tk), :]
        v = v_ref[pl.ds(off, tk), :]
        cols = []
        for c0 in range(0, tq, LANES):
            r0 = jnp.clip(off - q0 - c0, lo, BIAS_FAR) - lo
            cols.append(bias_ref[pl.ds(pl.multiple_of(r0, LANES), tk), :])
        bias = jnp.concatenate(cols, axis=1)
        _softmax_step_t(k, q1, v, m1, l1, a1, bias)
        _softmax_step_t(k, q2, v, m2, l2, a2, bias)
        return carry

    lax.fori_loop(0, nk, step, 0)
    o = a1[...] / l1[...] - lam_ref[0] * (a2[...] / l2[...])
    y = o * lax.rsqrt(jnp.mean(o * o, axis=0, keepdims=True) + NORM_EPS)
    o_ref[...] = jnp.transpose((y * g_ref[...]) * out_scale).astype(o_ref.dtype)


def _rel_bucket(rel):
    half = N_BUCKETS // 2
    exact = half // 2
    n = jnp.abs(rel)
    nf = jnp.maximum(n, 1).astype(F32)
    large = exact + (jnp.log(nf / exact) / math.log(REL_MAX_DIST / exact) * (half - exact)).astype(jnp.int32)
    large = jnp.minimum(large, half - 1)
    return jnp.where(rel > 0, half, 0) + jnp.where(n < exact, n, large)


def _skew(vec, nrows):
    p = vec.shape[-1]
    flat = jnp.tile(vec, (1,) * (vec.ndim - 1) + (nrows,))[..., :nrows * (p - 1)]
    return flat.reshape(vec.shape[:-1] + (nrows, p - 1))


def _diff_bias_geometry(tk):
    lo = -(REL_MAX_DIST + tk)
    return lo, BIAS_FAR - lo + tk


def _diagonal_table(tbl, tk):
    lo, nrows = _diff_bias_geometry(tk)
    rel = jnp.arange(lo - (LANES - 1), lo + nrows + 1, dtype=jnp.int32)
    vec = jnp.transpose(tbl[_rel_bucket(rel)]).astype(F32) * LOG2E
    diag = _skew(vec, LANES)[:, :, LANES - 1:LANES - 1 + nrows]
    return jnp.transpose(diag, (0, 2, 1))


def diff_attention(q, k, v, table, lam, out_g, *, row0, nseq, seq, lam_init):
    tq, tk = _attn_tiles(seq)
    nq = seq // tq
    qb0 = row0 // tq
    sb0 = row0 // seq
    lo, nrows = _diff_bias_geometry(tk)
    assert table.shape == (H_A, nrows, LANES)
    return pl.pallas_call(
        functools.partial(_diff_body, tq=tq, tk=tk, nk=seq // tk, lo=lo, out_scale=1.0 - lam_init),
        out_shape=jax.ShapeDtypeStruct((nseq * seq, H_A * DV_A), BF16),
        grid=(nseq, H_A, nq),
        in_specs=[pl.BlockSpec(memory_space=pltpu.SMEM),
                  pl.BlockSpec((tq, 2 * DK_A), lambda b, h, i: (qb0 + b * nq + i, h)),
                  pl.BlockSpec((seq, 2 * DK_A), lambda b, h, i: (sb0 + b, h)),
                  pl.BlockSpec((seq, DV_A), lambda b, h, i: (sb0 + b, h)),
                  pl.BlockSpec((None, nrows, LANES), lambda b, h, i: (h, 0, 0)),
                  pl.BlockSpec((DV_A, 1), lambda b, h, i: (0, 0))],
        out_specs=pl.BlockSpec((tq, DV_A), lambda b, h, i: (b * nq + i, h)),
        scratch_shapes=_attn_scratch(tq, DV_A) + _attn_scratch(tq, DV_A),
        compiler_params=_cparams(("parallel", "parallel", "parallel")),
        name="diff_attention",
    )(lam, q, k, v, table, out_g.reshape(DV_A, 1).astype(F32))


def _band_body(q_ref, kp_ref, kc_ref, kn_ref, vp_ref, vc_ref, vn_ref, bias_ref, o_ref, lse_ref,
               *, tu, nb):
    n = pl.program_id(2)
    col = lax.broadcasted_iota(jnp.int32, (1, tu + 2 * HALO), 1)
    outside = ((col < HALO) & (n == 0)) | ((col >= tu + HALO) & (n == nb - 1))
    for h in range(H_D):
        hs = slice(h * DH_D, (h + 1) * DH_D)
        k = jnp.concatenate([kp_ref[:, hs], kc_ref[:, hs], kn_ref[:, hs]], axis=0)
        v = jnp.concatenate([vp_ref[:, hs], vc_ref[:, hs], vn_ref[:, hs]], axis=0)
        s = _kq(q_ref[:, hs], k) + bias_ref[h]
        s = jnp.where(outside, NEG_INF, s)
        m = jnp.max(s, axis=1, keepdims=True)
        p = jnp.exp2(s - m)
        l = jnp.sum(p, axis=1, keepdims=True)
        o = jnp.dot(p.astype(BF16), v, preferred_element_type=F32) / l
        o_ref[:, hs] = o.astype(o_ref.dtype)
        lse_ref[:, hs] = jnp.broadcast_to(m + jnp.log2(l), (tu, DH_D))


def _band_table(bias_d, tu):
    p = 2 * tu + 2 * HALO + 1
    x = jnp.arange(p, dtype=jnp.int32)
    vecs = []
    for g, (_, dil) in enumerate(DILATIONS):
        tbl = bias_d[:, g * H_D:(g + 1) * H_D]
        band = jnp.transpose(tbl[_rel_bucket((x - HALO) * dil)]).astype(F32) * LOG2E
        vecs.append(jnp.where((x <= 2 * HALO)[None], band, NEG_INF))
    return _skew(jnp.stack(vecs), tu)[..., :tu + 2 * HALO]


def band_attention(q, k, v, table, g, *, row0, nseq, seq):
    _, dil = DILATIONS[g]
    length = seq // dil
    tu = table.shape[2]
    nb = length // tu
    m = q.shape[0]
    width = H_D * DH_D
    qv = q.reshape(m // dil, dil * N_DIL * width)
    kv = k.reshape(m // dil, dil * width)
    vv = v.reshape(m // dil, dil * width)
    u0 = row0 // dil
    hpt = tu // HALO
    last_halo = m // dil // HALO - 1

    def tile(b, n):
        return u0 // tu + b * nb + n

    cur = pl.BlockSpec((tu, width), lambda b, r, n: (tile(b, n), r))
    prev = pl.BlockSpec((HALO, width), lambda b, r, n: (jnp.maximum(tile(b, n) * hpt - 1, 0), r))
    nxt = pl.BlockSpec((HALO, width), lambda b, r, n: (jnp.minimum((tile(b, n) + 1) * hpt, last_halo), r))
    mo = nseq * seq
    out_spec = pl.BlockSpec((tu, width), lambda b, r, n: (b * nb + n, r))
    o, lse = pl.pallas_call(
        functools.partial(_band_body, tu=tu, nb=nb),
        out_shape=(jax.ShapeDtypeStruct((mo // dil, dil * width), F32),
                   jax.ShapeDtypeStruct((mo // dil, dil * width), F32)),
        grid=(nseq, dil, nb),
        in_specs=[pl.BlockSpec((tu, width), lambda b, r, n: (tile(b, n), r * N_DIL + g)),
                  prev, cur, nxt, prev, cur, nxt,
                  pl.BlockSpec((None, H_D, tu, tu + 2 * HALO), lambda b, r, n: (g, 0, 0, 0))],
        out_specs=(out_spec, out_spec),
        compiler_params=_cparams(("parallel", "parallel", "parallel")),
        name="band_attention",
    )(qv, kv, kv, kv, vv, vv, vv, table)
    return o.reshape(mo, width), lse.reshape(mo, width)


def _mix_body(*refs):
    o_refs = refs[:N_DIL]
    l_refs = refs[N_DIL:2 * N_DIL]
    out_ref = refs[2 * N_DIL]
    ls = [r[...] for r in l_refs]
    mx = functools.reduce(jnp.maximum, ls)
    es = [jnp.exp2(l - mx) for l in ls]
    den = functools.reduce(lambda a, b: a + b, es)
    acc = None
    for e, o_ref in zip(es, o_refs):
        term = (e / den) * o_ref[...]
        acc = term if acc is None else acc + term
    out_ref[...] = acc.astype(out_ref.dtype)


def band_mixture(outs, lses):
    m, c = outs[0].shape
    tm = _pick(m, (1024, 512, 256, 128, 64, 8))
    spec = pl.BlockSpec((tm, c), lambda i: (i, 0))
    return pl.pallas_call(
        _mix_body,
        out_shape=jax.ShapeDtypeStruct((m, c), BF16),
        grid=(m // tm,),
        in_specs=[spec] * (2 * N_DIL),
        out_specs=spec,
        compiler_params=_cparams(("parallel",)),
        name="band_mixture",
    )(*outs, *lses)


def kernel(x_prompt, x_sample, p_prompt, p_sample, rel_bias, norm_mix, w_in, a_q_norm, a_k_norm, a_lambda_q1, a_lambda_k1, a_lambda_q2, a_lambda_k2, a_out_norm, b_cq_norm, b_ckv_norm, b_w_uq, b_w_ukv, b_q_norm, b_k_norm, c_q_norm, c_k_norm, d_q_norm, d_k_norm, w_gate, w_branch, w_out, norm_ffn, w_ff1, w_ff2, norm_ple, w_ple_gate, w_ple_proj):
    depth = w_in.shape[0]
    d_model = x_prompt.shape[-1]
    groups = [(x_prompt.shape[0], x_prompt.shape[1]), (x_sample.shape[0], x_sample.shape[1])]
    row0s = [0, groups[0][0] * groups[0][1]]
    x = jnp.concatenate([x_prompt.reshape(-1, d_model), x_sample.reshape(-1, d_model)], axis=0)
    p_parts = [p_prompt.reshape(depth, -1, p_prompt.shape[-1]), p_sample.reshape(depth, -1, p_sample.shape[-1])]

    in_sizes = (512, 512, 512, 512, 512, 64, 512, 256, 256, 1536, 512, 512)
    offs = np.concatenate([[0], np.cumsum(in_sizes)])
    seg = lambda w, a, b: w[:, offs[a]:offs[b]]

    table_a = {}
    tables_d = {}
    for _, seq in groups:
        tk = _attn_tiles(seq)[1]
        if tk not in table_a:
            table_a[tk] = _diagonal_table(rel_bias[:, :H_A], tk)
        for _, dil in DILATIONS:
            tu = min(BAND_TU, seq // dil)
            if tu not in tables_d:
                tables_d[tu] = _band_table(rel_bias[:, H_A:], tu)

    two = lambda g: jnp.concatenate([g, g])
    pad_b = lambda g: jnp.pad(g, (0, DKP_B - DN_B - DR_B))
    for i in range(depth):
        lam_init = 0.8 - 0.6 * math.exp(-0.3 * i)
        wi = w_in[i]
        w_in_p = jnp.concatenate(
            [seg(wi, 0, 3), seg(wi, 6, 9), seg(wi, 9, 12), seg(wi, 3, 5), seg(wi, 5, 6)], axis=1)
        w_in_p = jnp.pad(w_in_p, ((0, 0), (0, N_IN - w_in_p.shape[1]))).astype(BF16)
        w_uq_p = jnp.pad(b_w_uq[i].reshape(Q_RANK_B, H_B, DN_B + DR_B),
                         ((0, 0), (0, 0), (0, DKP_B - DN_B - DR_B))).reshape(Q_RANK_B, H_B * DKP_B).astype(BF16)
        zeros = jnp.zeros((LANES,), F32)
        gains128 = jnp.stack([two(a_q_norm[i]) * (DK_A ** -0.5 * LOG2E), two(a_k_norm[i]),
                              c_q_norm[i] * (DH_C ** -0.5 * LOG2E), c_k_norm[i],
                              d_q_norm[i] * (DH_D ** -0.5 * LOG2E), d_k_norm[i], zeros, zeros]).astype(F32)
        gains512 = jnp.stack([b_cq_norm[i], b_ckv_norm[i],
                              jnp.concatenate([pad_b(b_q_norm[i]) * ((DN_B + DR_B) ** -0.5 * LOG2E),
                                               pad_b(b_k_norm[i])])]
                             + [jnp.zeros((512,), F32)] * 5).astype(F32)
        lam = (jnp.exp(jnp.sum(a_lambda_q1[i].astype(F32) * a_lambda_k1[i].astype(F32)))
               - jnp.exp(jnp.sum(a_lambda_q2[i].astype(F32) * a_lambda_k2[i].astype(F32))) + lam_init)
        lam = lam.reshape(1).astype(F32)

        h = rmsnorm(x, norm_mix[i])
        proj = matmul(h, w_in_p, out_dtype=F32)
        qa, ka, va, qb, kb, vb, qc, kc, vc, qd, kd, vd = prepare_mixer_inputs(
            proj, groups, gains128, gains512, w_uq_p, b_w_ukv[i].astype(BF16))

        branches = [[], [], [], []]
        for (nseq, seq), row0 in zip(groups, row0s):
            kw = dict(row0=row0, nseq=nseq, seq=seq)
            branches[0].append(diff_attention(qa, ka, va, table_a[_attn_tiles(seq)[1]], lam, a_out_norm[i],
                                              lam_init=lam_init, **kw))
            branches[1].append(dense_attention(qb, kb, vb, hq=H_B, hk=H_B, dk=DKP_B, dv=DV_B, **kw))
            branches[2].append(dense_attention(qc, kc, vc, hq=H_C, hk=KV_C, dk=DH_C, dv=DH_C, **kw))
            outs, lses = zip(*[band_attention(qd, kd, vd, tables_d[min(BAND_TU, seq // dil)], g, **kw)
                               for g, (_, dil) in enumerate(DILATIONS)])
            branches[3].append(band_mixture(outs, lses))

        merged = gated_merge(h, w_gate[i].astype(BF16), w_branch[i].astype(BF16), branches, out_dtype=BF16)
        x = matmul(merged, w_out[i].astype(BF16), out_dtype=F32, res=x)
        h2 = rmsnorm(x, norm_ffn[i])
        u = matmul(h2, w_ff1[i].astype(BF16), out_dtype=BF16, act="relu2")
        x = matmul(u, w_ff2[i].astype(BF16), out_dtype=F32, res=x)
        h3 = rmsnorm(x, norm_ple[i])
        x = gated_merge(h3, w_ple_gate[i].astype(BF16)[None], w_ple_proj[i].astype(BF16)[None],
                        [[p[i] for p in p_parts]], out_dtype=F32, res=x)

    y_prompt = x[:row0s[1]].reshape(x_prompt.shape)
    y_sample = x[row0s[1]:].reshape(x_sample.shape)
    return (y_prompt, y_sample)
```

```python
import functools
import math

import jax
import jax.numpy as jnp
import numpy as np
from jax import lax
from jax.experimental import pallas as pl
from jax.experimental.pallas import tpu as pltpu

BF16 = jnp.bfloat16
F32 = jnp.float32

GRID_W = 64
H_A, DK_A, DV_A = 4, 64, 128
H_B, DN_B, DR_B, DV_B = 4, 128, 64, 128
Q_RANK_B = KV_RANK_B = 512
H_C, KV_C, DH_C = 4, 2, 128
H_D, DH_D = 4, 128
DILATIONS = ((128, 1), (512, 4), (2048, 16))
N_DIL = len(DILATIONS)
HALO = 64
assert all(w // (2 * d) == HALO for w, d in DILATIONS)
N_BUCKETS = 32
REL_MAX_DIST = 1024
ROPE_THETA = 10000.0
NORM_EPS = 1e-6
NEG_INF = -1e30
LOG2E = 1.4426950408889634
LANES = 128
DKP_B = 256
VMEM_LIMIT = 56 * 1024 * 1024

MM_TM, MM_TN, MM_TK = 1024, 1024, 2048
ROW_TM = 512
GATED_TN = 256
GATED_ROW_TM = 256
ATTN_TQ, ATTN_TK = 2048, 512
PREP_TM = 256
BAND_TU = 256

COL_A, COL_C, COL_D, COL_BCQ, COL_BCKV, N_IN = 0, 1536, 2560, 5120, 5632, 6144


def _cparams(sem):
    return pltpu.CompilerParams(dimension_semantics=sem, vmem_limit_bytes=VMEM_LIMIT)


def _pick(n, prefs):
    for t in prefs:
        if n % t == 0:
            return t
    return n


def _rmsnorm_body(x_ref, g_ref, o_ref):
    x = x_ref[...]
    y = x * lax.rsqrt(jnp.mean(x * x, axis=-1, keepdims=True) + NORM_EPS)
    o_ref[...] = (y * g_ref[...]).astype(o_ref.dtype)


def rmsnorm(x, g, out_dtype=BF16):
    m, d = x.shape
    tm = _pick(m, (512, 256, 128, 64, 8))
    return pl.pallas_call(
        _rmsnorm_body,
        out_shape=jax.ShapeDtypeStruct((m, d), out_dtype),
        grid=(m // tm,),
        in_specs=[pl.BlockSpec((tm, d), lambda i: (i, 0)),
                  pl.BlockSpec((1, d), lambda i: (0, 0))],
        out_specs=pl.BlockSpec((tm, d), lambda i: (i, 0)),
        compiler_params=_cparams(("parallel",)),
        name="rmsnorm",
    )(x, g.reshape(1, d).astype(F32))


def _store_with_norm(acc, g_ref, o_ref, h_ref):
    o_ref[...] = acc
    inv = lax.rsqrt(jnp.mean(acc * acc, axis=-1, keepdims=True) + NORM_EPS)
    h_ref[...] = (acc * inv * g_ref[...]).astype(h_ref.dtype)


def _mm_body(*refs, act, has_res, has_norm, nk):
    a_ref, w_ref = refs[:2]
    n_in = 2 + has_res + has_norm
    r_ref = refs[2] if has_res else None
    g_ref = refs[n_in - 1] if has_norm else None
    o_ref = refs[n_in]
    h_ref = refs[n_in + 1] if has_norm else None
    scratch = refs[n_in + 1 + has_norm:]

    def finish(acc):
        if act == "relu2":
            acc = jnp.square(jnp.maximum(acc, 0.0))
        if has_res:
            acc = r_ref[...] + acc
        if has_norm:
            _store_with_norm(acc, g_ref, o_ref, h_ref)
        else:
            o_ref[...] = acc.astype(o_ref.dtype)

    part = jnp.dot(a_ref[...], w_ref[...], preferred_element_type=F32)
    if nk == 1:
        finish(part)
        return
    acc_ref, = scratch
    k = pl.program_id(2)

    @pl.when(k == 0)
    def _():
        acc_ref[...] = part

    @pl.when(k > 0)
    def _():
        acc_ref[...] += part

    @pl.when(k == nk - 1)
    def _():
        finish(acc_ref[...])


def matmul(a, w, *, out_dtype, act=None, res=None, norm_gain=None):
    m, kdim = a.shape
    n = w.shape[1]
    has_norm = norm_gain is not None
    tm = _pick(m, ((ROW_TM if has_norm else MM_TM), 512, 256, 128, 64, 8))
    tn = n if has_norm else _pick(n, (MM_TN, 512, 256, 128))
    tk = _pick(kdim, (MM_TK, 1024, 512))
    nk = kdim // tk
    in_specs = [pl.BlockSpec((tm, tk), lambda i, j, k: (i, k)),
                pl.BlockSpec((tk, tn), lambda i, j, k: (k, j))]
    args = [a, w]
    if res is not None:
        in_specs.append(pl.BlockSpec((tm, tn), lambda i, j, k: (i, j)))
        args.append(res)
    out_shape = jax.ShapeDtypeStruct((m, n), out_dtype)
    out_specs = pl.BlockSpec((tm, tn), lambda i, j, k: (i, j))
    if has_norm:
        in_specs.append(pl.BlockSpec((1, n), lambda i, j, k: (0, 0)))
        args.append(norm_gain.reshape(1, n).astype(F32))
        out_shape = (out_shape, jax.ShapeDtypeStruct((m, n), BF16))
        out_specs = (out_specs, pl.BlockSpec((tm, tn), lambda i, j, k: (i, j)))
    return pl.pallas_call(
        functools.partial(_mm_body, act=act, has_res=res is not None, has_norm=has_norm, nk=nk),
        out_shape=out_shape,
        grid=(m // tm, n // tn, nk),
        in_specs=in_specs,
        out_specs=out_specs,
        scratch_shapes=[pltpu.VMEM((tm, tn), F32)] if nk > 1 else [],
        compiler_params=_cparams(("parallel", "parallel", "arbitrary")),
        name="matmul",
    )(*args)


def _gated_body(*refs, nb, npart, split, has_res, has_norm):
    h_ref, wg_ref, wb_ref = refs[:3]
    br_refs = refs[3:3 + nb * npart]
    rest = list(refs[3 + nb * npart:])
    r_ref = rest.pop(0) if has_res else None
    g_ref = rest.pop(0) if has_norm else None
    o_ref = rest.pop(0)
    hn_ref = rest.pop(0) if has_norm else None
    h = h_ref[...]
    first = pl.program_id(0) < split
    acc = None
    for b in range(nb):
        parts = br_refs[b * npart:(b + 1) * npart]
        x = parts[0][...]
        if npart == 2:
            x = jnp.where(first, x, parts[1][...])
        gate = jnp.dot(h, wg_ref[b], preferred_element_type=F32)
        val = jnp.dot(x.astype(BF16), wb_ref[b], preferred_element_type=F32)
        term = jax.nn.sigmoid(gate) * val
        acc = term if acc is None else acc + term
    if has_res:
        acc = r_ref[...] + acc
    if has_norm:
        _store_with_norm(acc, g_ref, o_ref, hn_ref)
    else:
        o_ref[...] = acc.astype(o_ref.dtype)


def gated_merge(h, wg, wb, branches, *, out_dtype, res=None, norm_gain=None):
    m, d = h.shape
    nb, _, n = wg.shape
    kb = wb.shape[1]
    has_norm = norm_gain is not None
    tm = _pick(m, ((GATED_ROW_TM if has_norm else MM_TM), 256, 128, 64, 8))
    tn = n if has_norm else _pick(n, (GATED_TN, 128))
    npart = len(branches[0])
    rows0 = branches[0][0].shape[0]
    split = rows0 // tm if npart == 2 else m // tm
    assert npart in (1, 2) and rows0 % tm == 0
    in_specs = [pl.BlockSpec((tm, d), lambda i, j: (i, 0)),
                pl.BlockSpec((nb, d, tn), lambda i, j: (0, 0, j)),
                pl.BlockSpec((nb, kb, tn), lambda i, j: (0, 0, j))]
    args = [h, wg, wb]
    for parts in branches:
        in_specs.append(pl.BlockSpec((tm, kb), lambda i, j: (jnp.minimum(i, split - 1), 0)))
        if npart == 2:
            in_specs.append(pl.BlockSpec((tm, kb), lambda i, j: (jnp.maximum(i - split, 0), 0)))
        args.extend(parts)
    if res is not None:
        in_specs.append(pl.BlockSpec((tm, tn), lambda i, j: (i, j)))
        args.append(res)
    out_shape = jax.ShapeDtypeStruct((m, n), out_dtype)
    out_specs = pl.BlockSpec((tm, tn), lambda i, j: (i, j))
    if has_norm:
        in_specs.append(pl.BlockSpec((1, n), lambda i, j: (0, 0)))
        args.append(norm_gain.reshape(1, n).astype(F32))
        out_shape = (out_shape, jax.ShapeDtypeStruct((m, n), BF16))
        out_specs = (out_specs, pl.BlockSpec((tm, tn), lambda i, j: (i, j)))
    return pl.pallas_call(
        functools.partial(_gated_body, nb=nb, npart=npart, split=split, has_res=res is not None,
                          has_norm=has_norm),
        out_shape=out_shape,
        grid=(m // tm, n // tn),
        in_specs=in_specs,
        out_specs=out_specs,
        compiler_params=_cparams(("parallel", "parallel")),
        name="gated_merge",
    )(*args)


def _rope_rot(y, cos, sin_signed, first):
    partner = jnp.where(first, pltpu.roll(y, 96, 1), pltpu.roll(y, 32, 1))
    return y * cos + partner * sin_signed


def _inv_rms(ss, n):
    return lax.rsqrt(ss * (1.0 / n) + NORM_EPS)


def _rowsum(x):
    return jnp.sum(x, axis=1, keepdims=True)


def _prep_body(proj_ref, kr_ref, cosb_ref, sinb_ref, cosc_ref, sinc_ref, g128_ref, g512_ref, wuq_ref, wukv_ref,
               qa_ref, ka_ref, va_ref, qb_ref, kb_ref, vb_ref, qc_ref, kc_ref, vc_ref,
               qd0_ref, kd0_ref, vd0_ref, qd1_ref, kd1_ref, vd1_ref, qd2_ref, kd2_ref, vd2_ref,
               sq1, sq2, sk, sv):
    lane = lax.broadcasted_iota(jnp.int32, (1, LANES), 1)
    low = lane < DK_A
    first = (lane % 64) < 32
    blk = lambda c0, j: slice(c0 + j * LANES, c0 + (j + 1) * LANES)
    gain = lambda r: g128_ref[r:r + 1, :]

    for base, grow, dst in ((COL_A, 0, qa_ref), (COL_A + 512, 1, ka_ref)):
        for h in range(H_A):
            x = proj_ref[:, blk(base, h)]
            x2 = x * x
            r = jnp.where(low, _inv_rms(_rowsum(jnp.where(low, x2, 0.0)), DK_A),
                          _inv_rms(_rowsum(jnp.where(low, 0.0, x2)), DK_A))
            dst[:, blk(0, h)] = (x * r * gain(grow)).astype(BF16)
    va_ref[...] = proj_ref[:, COL_A + 1024:COL_A + 1536].astype(BF16)

    cosc, sinc = cosc_ref[...], sinc_ref[...]
    for base, nh, grow, dst in ((COL_C, H_C, 2, qc_ref), (COL_C + 512, KV_C, 3, kc_ref)):
        for h in range(nh):
            x = proj_ref[:, blk(base, h)]
            y = x * _inv_rms(_rowsum(x * x), DH_C) * gain(grow)
            dst[:, blk(0, h)] = _rope_rot(y, cosc, sinc, first).astype(BF16)
    vc_ref[...] = proj_ref[:, COL_C + 768:COL_C + 1024].astype(BF16)

    def head_norm(base, h, grow):
        x = proj_ref[:, blk(base, h)]
        return x * _inv_rms(_rowsum(x * x), DH_D) * gain(grow)

    tm = proj_ref.shape[0]
    for h in range(H_D):
        qd0_ref[0, :, blk(0, h)] = head_norm(COL_D, h, 4).astype(BF16)
        sq1[h] = head_norm(COL_D + 512, h, 4)
        sq2[h] = head_norm(COL_D + 1024, h, 4)
        sk[h] = head_norm(COL_D + 1536, h, 5)
        sv[h] = proj_ref[:, blk(COL_D + 2048, h)]
        kd0_ref[0, :, blk(0, h)] = sk[h].astype(BF16)
        vd0_ref[0, :, blk(0, h)] = sv[h].astype(BF16)
    for (_, dil), dsts in zip(DILATIONS[1:], (((sq1, qd1_ref), (sk, kd1_ref), (sv, vd1_ref)),
                                              ((sq2, qd2_ref), (sk, kd2_ref), (sv, vd2_ref)))):
        for src, dst in dsts:
            for h in range(H_D):
                for r in range(dil):
                    dst[r, :, blk(0, h)] = src[h, pl.ds(r, tm // dil, stride=dil), :].astype(BF16)

    cosb, sinb = cosb_ref[...], sinb_ref[...]
    cq = proj_ref[:, COL_BCQ:COL_BCQ + Q_RANK_B]
    cq = (cq * _inv_rms(_rowsum(cq * cq), Q_RANK_B) * g512_ref[0:1, :]).astype(BF16)
    ckv = proj_ref[:, COL_BCKV:COL_BCKV + KV_RANK_B]
    ckv = (ckv * _inv_rms(_rowsum(ckv * ckv), KV_RANK_B) * g512_ref[1:2, :]).astype(BF16)
    qraw = jnp.dot(cq, wuq_ref[...], preferred_element_type=F32)
    kvraw = jnp.dot(ckv, wukv_ref[...], preferred_element_type=F32)
    kr = kr_ref[...]
    kr_ss = _rowsum(kr * kr)
    gq_n, gq_r = g512_ref[2:3, 0:128], g512_ref[2:3, 128:256]
    gk_n, gk_r = g512_ref[2:3, 256:384], g512_ref[2:3, 384:512]
    for h in range(H_B):
        xn, xr = qraw[:, blk(0, 2 * h)], qraw[:, blk(0, 2 * h + 1)]
        r = _inv_rms(_rowsum(xn * xn) + _rowsum(xr * xr), DN_B + DR_B)
        qb_ref[:, blk(0, 2 * h)] = (xn * r * gq_n).astype(BF16)
        qb_ref[:, blk(0, 2 * h + 1)] = _rope_rot(xr * r * gq_r, cosb, sinb, first).astype(BF16)
        kn = kvraw[:, blk(0, 2 * h)]
        r = _inv_rms(_rowsum(kn * kn) + kr_ss, DN_B + DR_B)
        kb_ref[:, blk(0, 2 * h)] = (kn * r * gk_n).astype(BF16)
        kb_ref[:, blk(0, 2 * h + 1)] = _rope_rot(kr * r * gk_r, cosb, sinb, first).astype(BF16)
        vb_ref[:, blk(0, h)] = kvraw[:, blk(0, 2 * h + 1)].astype(BF16)


def _rope_lanes(pos):
    half = DR_B // 2
    inv = jnp.power(ROPE_THETA, -2.0 * jnp.arange(half, dtype=F32) / DR_B)
    ang = pos[:, None] * inv[None, :]
    cos, sin = jnp.cos(ang), jnp.sin(ang)
    return jnp.concatenate([cos, cos], axis=1), jnp.concatenate([-sin, sin], axis=1)


def prepare_mixer_inputs(proj, kr, groups, gains128, gains512, w_uq_p, w_ukv):
    m = proj.shape[0]
    tm = _pick(math.gcd(*[s for _, s in groups]), (PREP_TM, 128, 64, 8))
    smax = max(s for _, s in groups)
    pos = jnp.arange(smax, dtype=jnp.int32)
    cb, sb = _rope_lanes(pos.astype(F32))
    cosb = jnp.concatenate([cb, jnp.ones((smax, 64), F32)], axis=1)
    sinb = jnp.concatenate([sb, jnp.zeros((smax, 64), F32)], axis=1)
    cr, sr = _rope_lanes((pos // GRID_W).astype(F32))
    cc, sc = _rope_lanes((pos % GRID_W).astype(F32))
    cosc = jnp.concatenate([cr, cc], axis=1)
    sinc = jnp.concatenate([sr, sc], axis=1)
    (b0, s0), (_, s1) = groups
    t0, q1 = b0 * s0 // tm, s1 // tm

    def pos_block(i):
        return (jnp.where(i < t0, i % (s0 // tm), (i - t0) % q1), 0)

    row = lambda c: pl.BlockSpec((tm, c), lambda i: (i, 0))
    full = lambda a: pl.BlockSpec(a.shape, lambda i: (0,) * a.ndim)
    tab = pl.BlockSpec((tm, LANES), pos_block)
    wd = H_D * DH_D
    widths = (512, 512, 512, H_B * DKP_B, H_B * DKP_B, H_B * DV_B, 512, 256, 256)
    out_shape = [jax.ShapeDtypeStruct((m, c), BF16) for c in widths]
    out_specs = [row(c) for c in widths]
    for _, dil in DILATIONS:
        out_shape += [jax.ShapeDtypeStruct((dil, m // dil, wd), BF16)] * 3
        out_specs += [pl.BlockSpec((dil, tm // dil, wd), lambda i: (0, i, 0))] * 3
    return pl.pallas_call(
        _prep_body,
        out_shape=tuple(out_shape),
        grid=(m // tm,),
        in_specs=[row(N_IN), row(LANES), tab, tab, tab, tab, full(gains128), full(gains512),
                  full(w_uq_p), full(w_ukv)],
        out_specs=tuple(out_specs),
        scratch_shapes=[pltpu.VMEM((H_D, tm, DH_D), F32)] * 4,
        compiler_params=_cparams(("parallel",)),
        name="prepare_mixer_inputs",
    )(proj, kr, cosb, sinb, cosc, sinc, gains128, gains512, w_uq_p, w_ukv)


def _softmax_step_t(k, q, v, m_ref, l_ref, acc_ref, bias=None):
    st = _kq(k, q)
    if bias is not None:
        st = st + bias
    m_prev = m_ref[...]
    m_new = jnp.maximum(m_prev, jnp.max(st, axis=0, keepdims=True))
    alpha = jnp.exp2(m_prev - m_new)
    pt = jnp.exp2(st - m_new)
    l_ref[...] = alpha * l_ref[...] + jnp.sum(pt, axis=0, keepdims=True)
    pv = lax.dot_general(v, pt.astype(BF16), (((0,), (0,)), ((), ())), preferred_element_type=F32)
    acc_ref[...] = alpha * acc_ref[...] + pv
    m_ref[...] = m_new


def _init_stats(m_ref, l_ref, acc_ref):
    m_ref[...] = jnp.full(m_ref.shape, -jnp.inf, F32)
    l_ref[...] = jnp.zeros(l_ref.shape, F32)
    acc_ref[...] = jnp.zeros(acc_ref.shape, F32)


def _kq(k, q):
    return lax.dot_general(k, q, (((1,), (1,)), ((), ())), preferred_element_type=F32)


def _attn_body(q_ref, k_ref, v_ref, o_ref, m_ref, l_ref, acc_ref, *, tk, nk):
    _init_stats(m_ref, l_ref, acc_ref)
    q = q_ref[...]

    def step(j, carry):
        off = pl.multiple_of(j * tk, tk)
        _softmax_step_t(k_ref[pl.ds(off, tk), :], q, v_ref[pl.ds(off, tk), :], m_ref, l_ref, acc_ref)
        return carry

    lax.fori_loop(0, nk, step, 0)
    o_ref[...] = jnp.transpose(acc_ref[...] / l_ref[...]).astype(o_ref.dtype)


def _attn_scratch(tq, dv):
    return [pltpu.VMEM((1, tq), F32), pltpu.VMEM((1, tq), F32), pltpu.VMEM((dv, tq), F32)]


def _attn_tiles(seq):
    return _pick(seq, (ATTN_TQ, 1024, 512, 256, 128)), _pick(seq, (ATTN_TK, 256, 128))


def dense_attention(q, k, v, *, row0, nseq, seq, hq, hk, dk, dv):
    tq, tk = _attn_tiles(seq)
    nq = seq // tq
    group = hq // hk
    qb0 = row0 // tq
    sb0 = row0 // seq
    return pl.pallas_call(
        functools.partial(_attn_body, tk=tk, nk=seq // tk),
        out_shape=jax.ShapeDtypeStruct((nseq * seq, hq * dv), BF16),
        grid=(nseq, hq, nq),
        in_specs=[pl.BlockSpec((tq, dk), lambda b, h, i: (qb0 + b * nq + i, h)),
                  pl.BlockSpec((seq, dk), lambda b, h, i: (sb0 + b, h // group)),
                  pl.BlockSpec((seq, dv), lambda b, h, i: (sb0 + b, h // group))],
        out_specs=pl.BlockSpec((tq, dv), lambda b, h, i: (b * nq + i, h)),
        scratch_shapes=_attn_scratch(tq, dv),
        compiler_params=_cparams(("parallel", "parallel", "parallel")),
        name="dense_attention",
    )(q, k, v)


BIAS_FAR = 1152


def _diff_body(lam_ref, q_ref, k_ref, v_ref, bias_ref, g_ref, o_ref,
               m1, l1, a1, m2, l2, a2, *, tq, tk, nk, lo, out_scale):
    _init_stats(m1, l1, a1)
    _init_stats(m2, l2, a2)
    q = q_ref[...]
    lane = lax.broadcasted_iota(jnp.int32, q.shape, 1)
    zero = jnp.zeros_like(q)
    q1 = jnp.where(lane < DK_A, q, zero)
    q2 = jnp.where(lane >= DK_A, q, zero)
    q0 = pl.program_id(2) * tq

    def step(j, carry):
        off = pl.multiple_of(j * tk, tk)
        k = k_ref[pl.ds(off, tk), :]
        v = v_ref[pl.ds(off, tk), :]
        cols = []
        for c0 in range(0, tq, LANES):
            r0 = jnp.clip(off - q0 - c0, lo, BIAS_FAR) - lo
            cols.append(bias_ref[pl.ds(pl.multiple_of(r0, LANES), tk), :])
        bias = jnp.concatenate(cols, axis=1)
        _softmax_step_t(k, q1, v, m1, l1, a1, bias)
        _softmax_step_t(k, q2, v, m2, l2, a2, bias)
        return carry

    lax.fori_loop(0, nk, step, 0)
    o = a1[...] / l1[...] - lam_ref[0] * (a2[...] / l2[...])
    y = o * lax.rsqrt(jnp.mean(o * o, axis=0, keepdims=True) + NORM_EPS)
    o_ref[...] = jnp.transpose((y * g_ref[...]) * out_scale).astype(o_ref.dtype)


def _rel_bucket(rel):
    half = N_BUCKETS // 2
    exact = half // 2
    n = jnp.abs(rel)
    nf = jnp.maximum(n, 1).astype(F32)
    large = exact + (jnp.log(nf / exact) / math.log(REL_MAX_DIST / exact) * (half - exact)).astype(jnp.int32)
    large = jnp.minimum(large, half - 1)
    return jnp.where(rel > 0, half, 0) + jnp.where(n < exact, n, large)


def _skew(vec, nrows):
    p = vec.shape[-1]
    flat = jnp.tile(vec, (1,) * (vec.ndim - 1) + (nrows,))[..., :nrows * (p - 1)]
    return flat.reshape(vec.shape[:-1] + (nrows, p - 1))


def _diff_bias_geometry(tk):
    lo = -(REL_MAX_DIST + tk)
    return lo, BIAS_FAR - lo + tk


def _diagonal_table(tbl, tk):
    lo, nrows = _diff_bias_geometry(tk)
    rel = jnp.arange(lo - (LANES - 1), lo + nrows + 1, dtype=jnp.int32)
    vec = jnp.transpose(tbl[_rel_bucket(rel)]).astype(F32) * LOG2E
    diag = _skew(vec, LANES)[:, :, LANES - 1:LANES - 1 + nrows]
    return jnp.transpose(diag, (0, 2, 1))


def diff_attention(q, k, v, table, lam, out_g, *, row0, nseq, seq, lam_init):
    tq, tk = _attn_tiles(seq)
    nq = seq // tq
    qb0 = row0 // tq
    sb0 = row0 // seq
    lo, nrows = _diff_bias_geometry(tk)
    assert table.shape == (H_A, nrows, LANES)
    return pl.pallas_call(
        functools.partial(_diff_body, tq=tq, tk=tk, nk=seq // tk, lo=lo, out_scale=1.0 - lam_init),
        out_shape=jax.ShapeDtypeStruct((nseq * seq, H_A * DV_A), BF16),
        grid=(nseq, H_A, nq),
        in_specs=[pl.BlockSpec(memory_space=pltpu.SMEM),
                  pl.BlockSpec((tq, 2 * DK_A), lambda b, h, i: (qb0 + b * nq + i, h)),
                  pl.BlockSpec((seq, 2 * DK_A), lambda b, h, i: (sb0 + b, h)),
                  pl.BlockSpec((seq, DV_A), lambda b, h, i: (sb0 + b, h)),
                  pl.BlockSpec((None, nrows, LANES), lambda b, h, i: (h, 0, 0)),
                  pl.BlockSpec((DV_A, 1), lambda b, h, i: (0, 0))],
        out_specs=pl.BlockSpec((tq, DV_A), lambda b, h, i: (b * nq + i, h)),
        scratch_shapes=_attn_scratch(tq, DV_A) + _attn_scratch(tq, DV_A),
        compiler_params=_cparams(("parallel", "parallel", "parallel")),
        name="diff_attention",
    )(lam, q, k, v, table, out_g.reshape(DV_A, 1).astype(F32))


def _band_body(q_ref, kp_ref, kc_ref, kn_ref, vp_ref, vc_ref, vn_ref, bias_ref, o_ref, lse_ref,
               *, tu, nb):
    n = pl.program_id(2)
    col = lax.broadcasted_iota(jnp.int32, (1, tu + 2 * HALO), 1)
    outside = ((col < HALO) & (n == 0)) | ((col >= tu + HALO) & (n == nb - 1))
    for h in range(H_D):
        hs = slice(h * DH_D, (h + 1) * DH_D)
        k = jnp.concatenate([kp_ref[:, hs], kc_ref[:, hs], kn_ref[:, hs]], axis=0)
        v = jnp.concatenate([vp_ref[:, hs], vc_ref[:, hs], vn_ref[:, hs]], axis=0)
        s = _kq(q_ref[:, hs], k) + bias_ref[h]
        s = jnp.where(outside, NEG_INF, s)
        m = jnp.max(s, axis=1, keepdims=True)
        p = jnp.exp2(s - m)
        l = jnp.sum(p, axis=1, keepdims=True)
        o = jnp.dot(p.astype(BF16), v, preferred_element_type=F32) / l
        o_ref[:, hs] = o.astype(o_ref.dtype)
        lse_ref[:, hs] = jnp.broadcast_to(m + jnp.log2(l), (tu, DH_D))


def _band_table(bias_d, tu):
    p = 2 * tu + 2 * HALO + 1
    x = jnp.arange(p, dtype=jnp.int32)
    vecs = []
    for g, (_, dil) in enumerate(DILATIONS):
        tbl = bias_d[:, g * H_D:(g + 1) * H_D]
        band = jnp.transpose(tbl[_rel_bucket((x - HALO) * dil)]).astype(F32) * LOG2E
        vecs.append(jnp.where((x <= 2 * HALO)[None], band, NEG_INF))
    return _skew(jnp.stack(vecs), tu)[..., :tu + 2 * HALO]


def band_attention(q, k, v, table, g, *, row0, nseq, seq):
    dil, rows, width = q.shape
    length = seq // dil
    tu = table.shape[2]
    nb = length // tu
    u0 = row0 // dil
    hpt = tu // HALO
    last_halo = rows // HALO - 1

    def tile(b, n):
        return u0 // tu + b * nb + n

    cur = pl.BlockSpec((None, tu, width), lambda b, r, n: (r, tile(b, n), 0))
    prev = pl.BlockSpec((None, HALO, width), lambda b, r, n: (r, jnp.maximum(tile(b, n) * hpt - 1, 0), 0))
    nxt = pl.BlockSpec((None, HALO, width),
                       lambda b, r, n: (r, jnp.minimum((tile(b, n) + 1) * hpt, last_halo), 0))
    out_rows = nseq * length
    out_spec = pl.BlockSpec((None, tu, width), lambda b, r, n: (r, b * nb + n, 0))
    return pl.pallas_call(
        functools.partial(_band_body, tu=tu, nb=nb),
        out_shape=(jax.ShapeDtypeStruct((dil, out_rows, width), F32),
                   jax.ShapeDtypeStruct((dil, out_rows, width), F32)),
        grid=(nseq, dil, nb),
        in_specs=[cur, prev, cur, nxt, prev, cur, nxt,
                  pl.BlockSpec((None, H_D, tu, tu + 2 * HALO), lambda b, r, n: (g, 0, 0, 0))],
        out_specs=(out_spec, out_spec),
        compiler_params=_cparams(("parallel", "parallel", "parallel")),
        name="band_attention",
    )(q, k, k, k, v, v, v, table)


def _mix_body(*refs):
    o_refs = refs[:N_DIL]
    l_refs = refs[N_DIL:2 * N_DIL]
    out_ref = refs[2 * N_DIL]
    scratch = refs[2 * N_DIL + 1:]
    tm = out_ref.shape[0]
    vals = [o_refs[0][0], l_refs[0][0]]
    for g, (_, dil) in enumerate(DILATIONS[1:], start=1):
        for src, dst in ((o_refs[g], scratch[2 * g - 2]), (l_refs[g], scratch[2 * g - 1])):
            for h in range(H_D):
                for r in range(dil):
                    dst[h, pl.ds(r, tm // dil, stride=dil), :] = src[r, :, h * DH_D:(h + 1) * DH_D]
            vals.append(jnp.concatenate([dst[h] for h in range(H_D)], axis=1))
    outs, ls = vals[0::2], vals[1::2]
    mx = functools.reduce(jnp.maximum, ls)
    es = [jnp.exp2(l - mx) for l in ls]
    den = functools.reduce(lambda a, b: a + b, es)
    acc = None
    for e, o in zip(es, outs):
        term = (e / den) * o
        acc = term if acc is None else acc + term
    out_ref[...] = acc.astype(out_ref.dtype)


def band_mixture(outs, lses):
    _, m, c = outs[0].shape
    tm = _pick(m, (PREP_TM, 128, 64))
    specs = [pl.BlockSpec((dil, tm // dil, c), lambda i: (0, i, 0)) for _, dil in DILATIONS]
    return pl.pallas_call(
        _mix_body,
        out_shape=jax.ShapeDtypeStruct((m, c), BF16),
        grid=(m // tm,),
        in_specs=specs * 2,
        out_specs=pl.BlockSpec((tm, c), lambda i: (i, 0)),
        scratch_shapes=[pltpu.VMEM((H_D, tm, DH_D), F32)] * (2 * (N_DIL - 1)),
        compiler_params=_cparams(("parallel",)),
        name="band_mixture",
    )(*outs, *lses)


def kernel(x_prompt, x_sample, p_prompt, p_sample, rel_bias, norm_mix, w_in, a_q_norm, a_k_norm, a_lambda_q1, a_lambda_k1, a_lambda_q2, a_lambda_k2, a_out_norm, b_cq_norm, b_ckv_norm, b_w_uq, b_w_ukv, b_q_norm, b_k_norm, c_q_norm, c_k_norm, d_q_norm, d_k_norm, w_gate, w_branch, w_out, norm_ffn, w_ff1, w_ff2, norm_ple, w_ple_gate, w_ple_proj):
    depth = w_in.shape[0]
    d_model = x_prompt.shape[-1]
    groups = [(x_prompt.shape[0], x_prompt.shape[1]), (x_sample.shape[0], x_sample.shape[1])]
    row0s = [0, groups[0][0] * groups[0][1]]
    x = jnp.concatenate([x_prompt.reshape(-1, d_model), x_sample.reshape(-1, d_model)], axis=0)
    p_parts = [p_prompt.reshape(depth, -1, p_prompt.shape[-1]), p_sample.reshape(depth, -1, p_sample.shape[-1])]

    in_sizes = (512, 512, 512, 512, 512, 64, 512, 256, 256, 1536, 512, 512)
    offs = np.concatenate([[0], np.cumsum(in_sizes)])
    seg = lambda w, a, b: w[:, offs[a]:offs[b]]

    table_a = {}
    tables_d = {}
    for _, seq in groups:
        tk = _attn_tiles(seq)[1]
        if tk not in table_a:
            table_a[tk] = _diagonal_table(rel_bias[:, :H_A], tk)
        for _, dil in DILATIONS:
            tu = min(BAND_TU, seq // dil)
            if tu not in tables_d:
                tables_d[tu] = _band_table(rel_bias[:, H_A:], tu)

    two = lambda g: jnp.concatenate([g, g])
    pad_b = lambda g: jnp.pad(g, (0, DKP_B - DN_B - DR_B))
    h = rmsnorm(x, norm_mix[0])
    for i in range(depth):
        lam_init = 0.8 - 0.6 * math.exp(-0.3 * i)
        wi = w_in[i]
        w_in_p = jnp.concatenate([seg(wi, 0, 3), seg(wi, 6, 9), seg(wi, 9, 12), seg(wi, 3, 5)], axis=1).astype(BF16)
        w_kr_p = jnp.pad(seg(wi, 5, 6), ((0, 0), (0, LANES - DR_B))).astype(BF16)
        w_uq_p = jnp.pad(b_w_uq[i].reshape(Q_RANK_B, H_B, DN_B + DR_B),
                         ((0, 0), (0, 0), (0, DKP_B - DN_B - DR_B))).reshape(Q_RANK_B, H_B * DKP_B).astype(BF16)
        zeros = jnp.zeros((LANES,), F32)
        gains128 = jnp.stack([two(a_q_norm[i]) * (DK_A ** -0.5 * LOG2E), two(a_k_norm[i]),
                              c_q_norm[i] * (DH_C ** -0.5 * LOG2E), c_k_norm[i],
                              d_q_norm[i] * (DH_D ** -0.5 * LOG2E), d_k_norm[i], zeros, zeros]).astype(F32)
        gains512 = jnp.stack([b_cq_norm[i], b_ckv_norm[i],
                              jnp.concatenate([pad_b(b_q_norm[i]) * ((DN_B + DR_B) ** -0.5 * LOG2E),
                                               pad_b(b_k_norm[i])])]
                             + [jnp.zeros((512,), F32)] * 5).astype(F32)
        lam = (jnp.exp(jnp.sum(a_lambda_q1[i].astype(F32) * a_lambda_k1[i].astype(F32)))
               - jnp.exp(jnp.sum(a_lambda_q2[i].astype(F32) * a_lambda_k2[i].astype(F32))) + lam_init)
        lam = lam.reshape(1).astype(F32)

        proj = matmul(h, w_in_p, out_dtype=F32)
        kr = matmul(h, w_kr_p, out_dtype=F32)
        (qa, ka, va, qb, kb, vb, qc, kc, vc, *qkv_d) = prepare_mixer_inputs(
            proj, kr, groups, gains128, gains512, w_uq_p, b_w_ukv[i].astype(BF16))

        branches = [[], [], [], []]
        for (nseq, seq), row0 in zip(groups, row0s):
            kw = dict(row0=row0, nseq=nseq, seq=seq)
            branches[0].append(diff_attention(qa, ka, va, table_a[_attn_tiles(seq)[1]], lam, a_out_norm[i],
                                              lam_init=lam_init, **kw))
            branches[1].append(dense_attention(qb, kb, vb, hq=H_B, hk=H_B, dk=DKP_B, dv=DV_B, **kw))
            branches[2].append(dense_attention(qc, kc, vc, hq=H_C, hk=KV_C, dk=DH_C, dv=DH_C, **kw))
            outs, lses = zip(*[band_attention(*qkv_d[3 * g:3 * g + 3], tables_d[min(BAND_TU, seq // dil)], g, **kw)
                               for g, (_, dil) in enumerate(DILATIONS)])
            branches[3].append(band_mixture(outs, lses))

        merged = gated_merge(h, w_gate[i].astype(BF16), w_branch[i].astype(BF16), branches, out_dtype=BF16)
        x, h2 = matmul(merged, w_out[i].astype(BF16), out_dtype=F32, res=x, norm_gain=norm_ffn[i])
        u = matmul(h2, w_ff1[i].astype(BF16), out_dtype=BF16, act="relu2")
        x = matmul(u, w_ff2[i].astype(BF16), out_dtype=F32, res=x)
        h3 = rmsnorm(x, norm_ple[i])
        ple = dict(out_dtype=F32, res=x)
        ple_args = (h3, w_ple_gate[i].astype(BF16)[None], w_ple_proj[i].astype(BF16)[None], [[p[i] for p in p_parts]])
        if i + 1 < depth:
            x, h = gated_merge(*ple_args, norm_gain=norm_mix[i + 1], **ple)
        else:
            x = gated_merge(*ple_args, **ple)

    y_prompt = x[:row0s[1]].reshape(x_prompt.shape)
    y_sample = x[row0s[1]:].reshape(x_sample.shape)
    return (y_prompt, y_sample)
```

```python
import functools
import math

import jax
import jax.numpy as jnp
import numpy as np
from jax import lax
from jax.experimental import pallas as pl
from jax.experimental.pallas import tpu as pltpu

BF16 = jnp.bfloat16
F32 = jnp.float32

GRID_W = 64
H_A, DK_A, DV_A = 4, 64, 128
H_B, DN_B, DR_B, DV_B = 4, 128, 64, 128
Q_RANK_B = KV_RANK_B = 512
H_C, KV_C, DH_C = 4, 2, 128
H_D, DH_D = 4, 128
DILATIONS = ((128, 1), (512, 4), (2048, 16))
N_DIL = len(DILATIONS)
HALO = 64
assert all(w // (2 * d) == HALO for w, d in DILATIONS)
N_BUCKETS = 32
REL_MAX_DIST = 1024
ROPE_THETA = 10000.0
NORM_EPS = 1e-6
NEG_INF = -1e30
LOG2E = 1.4426950408889634
LANES = 128
DKP_B = 256
VMEM_LIMIT = 56 * 1024 * 1024

MM_TM, MM_TN, MM_TK = 1024, 1024, 2048
ROW_TM = 512
GATED_TN = 256
GATED_ROW_TM = 256
ATTN_TQ, ATTN_TK = 2048, 1024
PREP_TM = 256
BAND_TU = 256

COL_A, COL_C, COL_D, COL_BCQ, COL_BCKV, N_IN = 0, 1536, 2560, 5120, 5632, 6144


def _cparams(sem):
    return pltpu.CompilerParams(dimension_semantics=sem, vmem_limit_bytes=VMEM_LIMIT)


def _pick(n, prefs):
    for t in prefs:
        if n % t == 0:
            return t
    return n


def _rmsnorm_body(x_ref, g_ref, o_ref):
    x = x_ref[...]
    y = x * lax.rsqrt(jnp.mean(x * x, axis=-1, keepdims=True) + NORM_EPS)
    o_ref[...] = (y * g_ref[...]).astype(o_ref.dtype)


def rmsnorm(x, g, out_dtype=BF16):
    m, d = x.shape
    tm = _pick(m, (512, 256, 128, 64, 8))
    return pl.pallas_call(
        _rmsnorm_body,
        out_shape=jax.ShapeDtypeStruct((m, d), out_dtype),
        grid=(m // tm,),
        in_specs=[pl.BlockSpec((tm, d), lambda i: (i, 0)),
                  pl.BlockSpec((1, d), lambda i: (0, 0))],
        out_specs=pl.BlockSpec((tm, d), lambda i: (i, 0)),
        compiler_params=_cparams(("parallel",)),
        name="rmsnorm",
    )(x, g.reshape(1, d).astype(F32))


def _store_with_norm(acc, g_ref, o_ref, h_ref):
    o_ref[...] = acc
    inv = lax.rsqrt(jnp.mean(acc * acc, axis=-1, keepdims=True) + NORM_EPS)
    h_ref[...] = (acc * inv * g_ref[...]).astype(h_ref.dtype)


def _mm_body(*refs, act, has_res, has_norm, nk):
    a_ref, w_ref = refs[:2]
    n_in = 2 + has_res + has_norm
    r_ref = refs[2] if has_res else None
    g_ref = refs[n_in - 1] if has_norm else None
    o_ref = refs[n_in]
    h_ref = refs[n_in + 1] if has_norm else None
    scratch = refs[n_in + 1 + has_norm:]

    def finish(acc):
        if act == "relu2":
            acc = jnp.square(jnp.maximum(acc, 0.0))
        if has_res:
            acc = r_ref[...] + acc
        if has_norm:
            _store_with_norm(acc, g_ref, o_ref, h_ref)
        else:
            o_ref[...] = acc.astype(o_ref.dtype)

    part = jnp.dot(a_ref[...], w_ref[...], preferred_element_type=F32)
    if nk == 1:
        finish(part)
        return
    acc_ref, = scratch
    k = pl.program_id(2)

    @pl.when(k == 0)
    def _():
        acc_ref[...] = part

    @pl.when(k > 0)
    def _():
        acc_ref[...] += part

    @pl.when(k == nk - 1)
    def _():
        finish(acc_ref[...])


def matmul(a, w, *, out_dtype, act=None, res=None, norm_gain=None, tn=MM_TN):
    m, kdim = a.shape
    n = w.shape[1]
    has_norm = norm_gain is not None
    tm = _pick(m, ((ROW_TM if has_norm else MM_TM), 512, 256, 128, 64, 8))
    tn = n if has_norm else _pick(n, (tn, 512, 256, 128))
    tk = _pick(kdim, (MM_TK, 1024, 512))
    nk = kdim // tk
    in_specs = [pl.BlockSpec((tm, tk), lambda i, j, k: (i, k)),
                pl.BlockSpec((tk, tn), lambda i, j, k: (k, j))]
    args = [a, w]
    if res is not None:
        in_specs.append(pl.BlockSpec((tm, tn), lambda i, j, k: (i, j)))
        args.append(res)
    out_shape = jax.ShapeDtypeStruct((m, n), out_dtype)
    out_specs = pl.BlockSpec((tm, tn), lambda i, j, k: (i, j))
    if has_norm:
        in_specs.append(pl.BlockSpec((1, n), lambda i, j, k: (0, 0)))
        args.append(norm_gain.reshape(1, n).astype(F32))
        out_shape = (out_shape, jax.ShapeDtypeStruct((m, n), BF16))
        out_specs = (out_specs, pl.BlockSpec((tm, tn), lambda i, j, k: (i, j)))
    return pl.pallas_call(
        functools.partial(_mm_body, act=act, has_res=res is not None, has_norm=has_norm, nk=nk),
        out_shape=out_shape,
        grid=(m // tm, n // tn, nk),
        in_specs=in_specs,
        out_specs=out_specs,
        scratch_shapes=[pltpu.VMEM((tm, tn), F32)] if nk > 1 else [],
        compiler_params=_cparams(("parallel", "parallel", "arbitrary")),
        name="matmul",
    )(*args)


def _gated_body(*refs, nb, npart, split, has_res, has_norm):
    h_ref, wg_ref, wb_ref = refs[:3]
    br_refs = refs[3:3 + nb * npart]
    rest = list(refs[3 + nb * npart:])
    r_ref = rest.pop(0) if has_res else None
    g_ref = rest.pop(0) if has_norm else None
    o_ref = rest.pop(0)
    hn_ref = rest.pop(0) if has_norm else None
    h = h_ref[...]
    first = pl.program_id(0) < split
    acc = None
    for b in range(nb):
        parts = br_refs[b * npart:(b + 1) * npart]
        x = parts[0][...]
        if npart == 2:
            x = jnp.where(first, x, parts[1][...])
        gate = jnp.dot(h, wg_ref[b], preferred_element_type=F32)
        val = jnp.dot(x.astype(BF16), wb_ref[b], preferred_element_type=F32)
        term = jax.nn.sigmoid(gate) * val
        acc = term if acc is None else acc + term
    if has_res:
        acc = r_ref[...] + acc
    if has_norm:
        _store_with_norm(acc, g_ref, o_ref, hn_ref)
    else:
        o_ref[...] = acc.astype(o_ref.dtype)


def gated_merge(h, wg, wb, branches, *, out_dtype, res=None, norm_gain=None, full_rows=False, row0=0, nrows=None):
    d = h.shape[1]
    m = h.shape[0] if nrows is None else nrows
    nb, _, n = wg.shape
    kb = wb.shape[1]
    has_norm = norm_gain is not None
    full_rows = full_rows or has_norm
    tm = _pick(math.gcd(m, row0), ((GATED_ROW_TM if full_rows else MM_TM), 256, 128, 64, 8))
    tn = n if full_rows else _pick(n, (GATED_TN, 128))
    i0 = row0 // tm
    npart = len(branches[0])
    rows0 = branches[0][0].shape[0]
    split = rows0 // tm if npart == 2 else m // tm
    assert npart in (1, 2) and rows0 % tm == 0
    in_specs = [pl.BlockSpec((tm, d), lambda i, j: (i0 + i, 0)),
                pl.BlockSpec((nb, d, tn), lambda i, j: (0, 0, j)),
                pl.BlockSpec((nb, kb, tn), lambda i, j: (0, 0, j))]
    args = [h, wg, wb]
    for parts in branches:
        in_specs.append(pl.BlockSpec((tm, kb), lambda i, j: (jnp.minimum(i, split - 1), 0)))
        if npart == 2:
            in_specs.append(pl.BlockSpec((tm, kb), lambda i, j: (jnp.maximum(i - split, 0), 0)))
        args.extend(parts)
    if res is not None:
        in_specs.append(pl.BlockSpec((tm, tn), lambda i, j: (i0 + i, j)))
        args.append(res)
    out_shape = jax.ShapeDtypeStruct((m, n), out_dtype)
    out_specs = pl.BlockSpec((tm, tn), lambda i, j: (i, j))
    if has_norm:
        in_specs.append(pl.BlockSpec((1, n), lambda i, j: (0, 0)))
        args.append(norm_gain.reshape(1, n).astype(F32))
        out_shape = (out_shape, jax.ShapeDtypeStruct((m, n), BF16))
        out_specs = (out_specs, pl.BlockSpec((tm, tn), lambda i, j: (i, j)))
    return pl.pallas_call(
        functools.partial(_gated_body, nb=nb, npart=npart, split=split, has_res=res is not None,
                          has_norm=has_norm),
        out_shape=out_shape,
        grid=(m // tm, n // tn),
        in_specs=in_specs,
        out_specs=out_specs,
        compiler_params=_cparams(("parallel", "parallel")),
        name="gated_merge",
    )(*args)


PAIR = 2 * LANES


def _rope_rot(y, cos, sin_signed):
    first = (lax.broadcasted_iota(jnp.int32, (1, LANES), 1) % 64) < 32
    parts = []
    for c0 in range(0, y.shape[1], LANES):
        yb = y[:, c0:c0 + LANES]
        parts.append(jnp.where(first, pltpu.roll(yb, 96, 1), pltpu.roll(yb, 32, 1)))
    return y * cos + jnp.concatenate(parts, axis=1) * sin_signed


def _inv_rms(ss, n):
    return lax.rsqrt(ss * (1.0 / n) + NORM_EPS)


def _group_ones(width):
    shift = width.bit_length() - 1
    r = lax.broadcasted_iota(jnp.int32, (PAIR, PAIR), 0) >> shift
    c = lax.broadcasted_iota(jnp.int32, (PAIR, PAIR), 1) >> shift
    return jnp.where(r == c, 1.0, 0.0).astype(BF16)


def _group_sumsq(x, ones):
    return jnp.dot((x * x).astype(BF16), ones, preferred_element_type=F32)


def _prep_body(proj_ref, kr_ref, cosb_ref, sinb_ref, cosc_ref, sinc_ref, g256_ref, g512_ref, wuq_ref, wukv_ref,
               qa_ref, ka_ref, va_ref, qb_ref, kb_ref, vb_ref, qc_ref, kc_ref, vc_ref,
               qd0_ref, kd0_ref, vd0_ref, qd1_ref, kd1_ref, vd1_ref, qd2_ref, kd2_ref, vd2_ref,
               sq1, sq2, sk, sv):
    tm = proj_ref.shape[0]
    ones64, ones128, ones256 = _group_ones(DK_A), _group_ones(LANES), _group_ones(PAIR)
    blk = lambda c0, j: slice(c0 + j * LANES, c0 + (j + 1) * LANES)
    pair = lambda c0, j: slice(c0 + j * PAIR, c0 + (j + 1) * PAIR)
    gain = lambda r: g256_ref[r:r + 1, :]
    twice = lambda t: jnp.concatenate([t, t], axis=1)

    def normed(c0, j, ones, n, grow):
        x = proj_ref[:, pair(c0, j)]
        return x * _inv_rms(_group_sumsq(x, ones), n) * gain(grow)

    for base, grow, dst in ((COL_A, 0, qa_ref), (COL_A + 512, 1, ka_ref)):
        for j in range(H_A // 2):
            dst[:, pair(0, j)] = normed(base, j, ones64, DK_A, grow).astype(BF16)
    va_ref[...] = proj_ref[:, COL_A + 1024:COL_A + 1536].astype(BF16)

    cosc, sinc = twice(cosc_ref[...]), twice(sinc_ref[...])
    for base, nh, grow, dst in ((COL_C, H_C, 2, qc_ref), (COL_C + 512, KV_C, 3, kc_ref)):
        for j in range(nh // 2):
            dst[:, pair(0, j)] = _rope_rot(normed(base, j, ones128, DH_C, grow), cosc, sinc).astype(BF16)
    vc_ref[...] = proj_ref[:, COL_C + 768:COL_C + 1024].astype(BF16)

    for j in range(H_D // 2):
        qd0_ref[0, :, pair(0, j)] = normed(COL_D, j, ones128, DH_D, 4).astype(BF16)
        yk = normed(COL_D + 1536, j, ones128, DH_D, 5)
        kd0_ref[0, :, pair(0, j)] = yk.astype(BF16)
        for scr, y in ((sq1, normed(COL_D + 512, j, ones128, DH_D, 4)),
                       (sq2, normed(COL_D + 1024, j, ones128, DH_D, 4)), (sk, yk)):
            scr[2 * j] = y[:, :LANES]
            scr[2 * j + 1] = y[:, LANES:]
    for h in range(H_D):
        sv[h] = proj_ref[:, blk(COL_D + 2048, h)]
    vd0_ref[0] = proj_ref[:, COL_D + 2048:COL_D + 2560].astype(BF16)
    for (_, dil), dsts in zip(DILATIONS[1:], (((sq1, qd1_ref), (sk, kd1_ref), (sv, vd1_ref)),
                                              ((sq2, qd2_ref), (sk, kd2_ref), (sv, vd2_ref)))):
        for src, dst in dsts:
            for h in range(H_D):
                for r in range(dil):
                    dst[r, :, blk(0, h)] = src[h, pl.ds(r, tm // dil, stride=dil), :].astype(BF16)

    def latent(c0, grow):
        x = proj_ref[:, c0:c0 + 2 * PAIR]
        ss = _group_sumsq(x[:, :PAIR], ones256) + _group_sumsq(x[:, PAIR:], ones256)
        return (x * twice(_inv_rms(ss, 2 * PAIR)) * g512_ref[grow:grow + 1, :]).astype(BF16)

    qraw = jnp.dot(latent(COL_BCQ, 0), wuq_ref[...], preferred_element_type=F32)
    kvraw = jnp.dot(latent(COL_BCKV, 1), wukv_ref[...], preferred_element_type=F32)
    kr = kr_ref[...]
    cosb, sinb = cosb_ref[...], sinb_ref[...]
    gq, gk = g512_ref[2:3, 0:PAIR], g512_ref[2:3, PAIR:2 * PAIR]
    for h in range(H_B):
        xq = qraw[:, pair(0, h)]
        xk = jnp.concatenate([kvraw[:, blk(0, 2 * h)], kr], axis=1)
        for x, g, dst in ((xq, gq, qb_ref), (xk, gk, kb_ref)):
            y = x * _inv_rms(_group_sumsq(x, ones256), DN_B + DR_B) * g
            dst[:, blk(0, 2 * h)] = y[:, :LANES].astype(BF16)
            dst[:, blk(0, 2 * h + 1)] = _rope_rot(y[:, LANES:], cosb, sinb).astype(BF16)
        vb_ref[:, blk(0, h)] = kvraw[:, blk(0, 2 * h + 1)].astype(BF16)


def _rope_lanes(pos):
    half = DR_B // 2
    inv = jnp.power(ROPE_THETA, -2.0 * jnp.arange(half, dtype=F32) / DR_B)
    ang = pos[:, None] * inv[None, :]
    cos, sin = jnp.cos(ang), jnp.sin(ang)
    return jnp.concatenate([cos, cos], axis=1), jnp.concatenate([-sin, sin], axis=1)


def prepare_mixer_inputs(proj, kr, groups, gains256, gains512, w_uq_p, w_ukv):
    m = proj.shape[0]
    tm = _pick(math.gcd(*[s for _, s in groups]), (PREP_TM, 128, 64, 8))
    smax = max(s for _, s in groups)
    pos = jnp.arange(smax, dtype=jnp.int32)
    cb, sb = _rope_lanes(pos.astype(F32))
    cosb = jnp.concatenate([cb, jnp.ones((smax, 64), F32)], axis=1)
    sinb = jnp.concatenate([sb, jnp.zeros((smax, 64), F32)], axis=1)
    cr, sr = _rope_lanes((pos // GRID_W).astype(F32))
    cc, sc = _rope_lanes((pos % GRID_W).astype(F32))
    cosc = jnp.concatenate([cr, cc], axis=1)
    sinc = jnp.concatenate([sr, sc], axis=1)
    (b0, s0), (_, s1) = groups
    t0, q1 = b0 * s0 // tm, s1 // tm

    def pos_block(i):
        return (jnp.where(i < t0, i % (s0 // tm), (i - t0) % q1), 0)

    row = lambda c: pl.BlockSpec((tm, c), lambda i: (i, 0))
    full = lambda a: pl.BlockSpec(a.shape, lambda i: (0,) * a.ndim)
    tab = pl.BlockSpec((tm, LANES), pos_block)
    wd = H_D * DH_D
    widths = (512, 512, 512, H_B * DKP_B, H_B * DKP_B, H_B * DV_B, 512, 256, 256)
    out_shape = [jax.ShapeDtypeStruct((m, c), BF16) for c in widths]
    out_specs = [row(c) for c in widths]
    for _, dil in DILATIONS:
        out_shape += [jax.ShapeDtypeStruct((dil, m // dil, wd), BF16)] * 3
        out_specs += [pl.BlockSpec((dil, tm // dil, wd), lambda i: (0, i, 0))] * 3
    return pl.pallas_call(
        _prep_body,
        out_shape=tuple(out_shape),
        grid=(m // tm,),
        in_specs=[row(N_IN), row(LANES), tab, tab, tab, tab, full(gains256), full(gains512),
                  full(w_uq_p), full(w_ukv)],
        out_specs=tuple(out_specs),
        scratch_shapes=[pltpu.VMEM((H_D, tm, DH_D), F32)] * 4,
        compiler_params=_cparams(("parallel",)),
        name="prepare_mixer_inputs",
    )(proj, kr, cosb, sinb, cosc, sinc, gains256, gains512, w_uq_p, w_ukv)


def _softmax_step_t(k, q, v, m_ref, l_ref, acc_ref, bias=None):
    st = _kq(k, q)
    if bias is not None:
        st = st + bias
    m_prev = m_ref[...]
    m_new = jnp.maximum(m_prev, jnp.max(st, axis=0, keepdims=True))
    alpha = jnp.exp2(m_prev - m_new)
    pt = jnp.exp2(st - m_new)
    l_ref[...] = alpha * l_ref[...] + jnp.sum(pt, axis=0, keepdims=True)
    pv = lax.dot_general(v, pt.astype(BF16), (((0,), (0,)), ((), ())), preferred_element_type=F32)
    acc_ref[...] = alpha * acc_ref[...] + pv
    m_ref[...] = m_new


def _init_stats(m_ref, l_ref, acc_ref):
    m_ref[...] = jnp.full(m_ref.shape, -jnp.inf, F32)
    l_ref[...] = jnp.zeros(l_ref.shape, F32)
    acc_ref[...] = jnp.zeros(acc_ref.shape, F32)


def _kq(k, q):
    return lax.dot_general(k, q, (((1,), (1,)), ((), ())), preferred_element_type=F32)


def _attn_body(q_ref, k_ref, v_ref, o_ref, m_ref, l_ref, acc_ref, *, tk, nk):
    _init_stats(m_ref, l_ref, acc_ref)
    q = q_ref[...]

    def step(j, carry):
        off = pl.multiple_of(j * tk, tk)
        _softmax_step_t(k_ref[pl.ds(off, tk), :], q, v_ref[pl.ds(off, tk), :], m_ref, l_ref, acc_ref)
        return carry

    lax.fori_loop(0, nk, step, 0)
    o_ref[...] = jnp.transpose(acc_ref[...] / l_ref[...]).astype(o_ref.dtype)


def _attn_scratch(tq, dv):
    return [pltpu.VMEM((1, tq), F32), pltpu.VMEM((1, tq), F32), pltpu.VMEM((dv, tq), F32)]


def _attn_tiles(seq):
    return _pick(seq, (ATTN_TQ, 1024, 512, 256, 128)), _pick(seq, (ATTN_TK, 256, 128))


def dense_attention(q, k, v, *, row0, nseq, seq, hq, hk, dk, dv):
    tq, tk = _attn_tiles(seq)
    nq = seq // tq
    group = hq // hk
    qb0 = row0 // tq
    sb0 = row0 // seq
    return pl.pallas_call(
        functools.partial(_attn_body, tk=tk, nk=seq // tk),
        out_shape=jax.ShapeDtypeStruct((nseq * seq, hq * dv), BF16),
        grid=(nseq, hq, nq),
        in_specs=[pl.BlockSpec((tq, dk), lambda b, h, i: (qb0 + b * nq + i, h)),
                  pl.BlockSpec((seq, dk), lambda b, h, i: (sb0 + b, h // group)),
                  pl.BlockSpec((seq, dv), lambda b, h, i: (sb0 + b, h // group))],
        out_specs=pl.BlockSpec((tq, dv), lambda b, h, i: (b * nq + i, h)),
        scratch_shapes=_attn_scratch(tq, dv),
        compiler_params=_cparams(("parallel", "parallel", "parallel")),
        name="dense_attention",
    )(q, k, v)


BIAS_FAR = 1152


def _diff_body(lam_ref, q_ref, k_ref, v_ref, bias_ref, g_ref, o_ref,
               m1, l1, a1, m2, l2, a2, *, tq, tk, nk, lo, out_scale):
    _init_stats(m1, l1, a1)
    _init_stats(m2, l2, a2)
    q = q_ref[...]
    lane = lax.broadcasted_iota(jnp.int32, q.shape, 1)
    zero = jnp.zeros_like(q)
    q1 = jnp.where(lane < DK_A, q, zero)
    q2 = jnp.where(lane >= DK_A, q, zero)
    q0 = pl.program_id(2) * tq

    def step(j, carry):
        off = pl.multiple_of(j * tk, tk)
        k = k_ref[pl.ds(off, tk), :]
        v = v_ref[pl.ds(off, tk), :]
        cols = []
        for c0 in range(0, tq, LANES):
            r0 = jnp.clip(off - q0 - c0, lo, BIAS_FAR) - lo
            cols.append(bias_ref[pl.ds(pl.multiple_of(r0, LANES), tk), :])
        bias = jnp.concatenate(cols, axis=1)
        _softmax_step_t(k, q1, v, m1, l1, a1, bias)
        _softmax_step_t(k, q2, v, m2, l2, a2, bias)
        return carry

    lax.fori_loop(0, nk, step, 0)
    o = a1[...] / l1[...] - lam_ref[0] * (a2[...] / l2[...])
    y = o * lax.rsqrt(jnp.mean(o * o, axis=0, keepdims=True) + NORM_EPS)
    o_ref[...] = jnp.transpose((y * g_ref[...]) * out_scale).astype(o_ref.dtype)


def _rel_bucket(rel):
    half = N_BUCKETS // 2
    exact = half // 2
    n = jnp.abs(rel)
    nf = jnp.maximum(n, 1).astype(F32)
    large = exact + (jnp.log(nf / exact) / math.log(REL_MAX_DIST / exact) * (half - exact)).astype(jnp.int32)
    large = jnp.minimum(large, half - 1)
    return jnp.where(rel > 0, half, 0) + jnp.where(n < exact, n, large)


def _skew(vec, nrows):
    p = vec.shape[-1]
    flat = jnp.tile(vec, (1,) * (vec.ndim - 1) + (nrows,))[..., :nrows * (p - 1)]
    return flat.reshape(vec.shape[:-1] + (nrows, p - 1))


def _diff_bias_geometry(tk):
    lo = -(REL_MAX_DIST + tk)
    return lo, BIAS_FAR - lo + tk


def _diagonal_table(tbl, tk):
    lo, nrows = _diff_bias_geometry(tk)
    rel = jnp.arange(lo - (LANES - 1), lo + nrows + 1, dtype=jnp.int32)
    vec = jnp.transpose(tbl[_rel_bucket(rel)]).astype(F32) * LOG2E
    diag = _skew(vec, LANES)[:, :, LANES - 1:LANES - 1 + nrows]
    return jnp.transpose(diag, (0, 2, 1))


def diff_attention(q, k, v, table, lam, out_g, *, row0, nseq, seq, lam_init):
    tq, tk = _attn_tiles(seq)
    nq = seq // tq
    qb0 = row0 // tq
    sb0 = row0 // seq
    lo, nrows = _diff_bias_geometry(tk)
    assert table.shape == (H_A, nrows, LANES)
    return pl.pallas_call(
        functools.partial(_diff_body, tq=tq, tk=tk, nk=seq // tk, lo=lo, out_scale=1.0 - lam_init),
        out_shape=jax.ShapeDtypeStruct((nseq * seq, H_A * DV_A), BF16),
        grid=(nseq, H_A, nq),
        in_specs=[pl.BlockSpec(memory_space=pltpu.SMEM),
                  pl.BlockSpec((tq, 2 * DK_A), lambda b, h, i: (qb0 + b * nq + i, h)),
                  pl.BlockSpec((seq, 2 * DK_A), lambda b, h, i: (sb0 + b, h)),
                  pl.BlockSpec((seq, DV_A), lambda b, h, i: (sb0 + b, h)),
                  pl.BlockSpec((None, nrows, LANES), lambda b, h, i: (h, 0, 0)),
                  pl.BlockSpec((DV_A, 1), lambda b, h, i: (0, 0))],
        out_specs=pl.BlockSpec((tq, DV_A), lambda b, h, i: (b * nq + i, h)),
        scratch_shapes=_attn_scratch(tq, DV_A) + _attn_scratch(tq, DV_A),
        compiler_params=_cparams(("parallel", "parallel", "parallel")),
        name="diff_attention",
    )(lam, q, k, v, table, out_g.reshape(DV_A, 1).astype(F32))


def _band_body(q_ref, kp_ref, kc_ref, kn_ref, vp_ref, vc_ref, vn_ref, bias_ref, o_ref, lse_ref,
               *, tu, nb):
    n = pl.program_id(2)
    col = lax.broadcasted_iota(jnp.int32, (1, tu + 2 * HALO), 1)
    outside = ((col < HALO) & (n == 0)) | ((col >= tu + HALO) & (n == nb - 1))
    for h in range(H_D):
        hs = slice(h * DH_D, (h + 1) * DH_D)
        k = jnp.concatenate([kp_ref[:, hs], kc_ref[:, hs], kn_ref[:, hs]], axis=0)
        v = jnp.concatenate([vp_ref[:, hs], vc_ref[:, hs], vn_ref[:, hs]], axis=0)
        s = _kq(q_ref[:, hs], k) + bias_ref[h]
        s = jnp.where(outside, NEG_INF, s)
        m = jnp.max(s, axis=1, keepdims=True)
        p = jnp.exp2(s - m)
        l = jnp.sum(p, axis=1, keepdims=True)
        o = jnp.dot(p.astype(BF16), v, preferred_element_type=F32) / l
        o_ref[:, hs] = o.astype(o_ref.dtype)
        lse_ref[:, hs] = jnp.broadcast_to(m + jnp.log2(l), (tu, DH_D))


def _band_table(bias_d, tu):
    p = 2 * tu + 2 * HALO + 1
    x = jnp.arange(p, dtype=jnp.int32)
    vecs = []
    for g, (_, dil) in enumerate(DILATIONS):
        tbl = bias_d[:, g * H_D:(g + 1) * H_D]
        band = jnp.transpose(tbl[_rel_bucket((x - HALO) * dil)]).astype(F32) * LOG2E
        vecs.append(jnp.where((x <= 2 * HALO)[None], band, NEG_INF))
    return _skew(jnp.stack(vecs), tu)[..., :tu + 2 * HALO]


def band_attention(q, k, v, table, g, *, row0, nseq, seq):
    dil, rows, width = q.shape
    length = seq // dil
    tu = table.shape[2]
    nb = length // tu
    u0 = row0 // dil
    hpt = tu // HALO
    last_halo = rows // HALO - 1

    def tile(b, n):
        return u0 // tu + b * nb + n

    cur = pl.BlockSpec((None, tu, width), lambda b, r, n: (r, tile(b, n), 0))
    prev = pl.BlockSpec((None, HALO, width), lambda b, r, n: (r, jnp.maximum(tile(b, n) * hpt - 1, 0), 0))
    nxt = pl.BlockSpec((None, HALO, width),
                       lambda b, r, n: (r, jnp.minimum((tile(b, n) + 1) * hpt, last_halo), 0))
    out_rows = nseq * length
    out_spec = pl.BlockSpec((None, tu, width), lambda b, r, n: (r, b * nb + n, 0))
    return pl.pallas_call(
        functools.partial(_band_body, tu=tu, nb=nb),
        out_shape=(jax.ShapeDtypeStruct((dil, out_rows, width), F32),
                   jax.ShapeDtypeStruct((dil, out_rows, width), F32)),
        grid=(nseq, dil, nb),
        in_specs=[cur, prev, cur, nxt, prev, cur, nxt,
                  pl.BlockSpec((None, H_D, tu, tu + 2 * HALO), lambda b, r, n: (g, 0, 0, 0))],
        out_specs=(out_spec, out_spec),
        compiler_params=_cparams(("parallel", "parallel", "parallel")),
        name="band_attention",
    )(q, k, k, k, v, v, v, table)


def _mix_body(*refs):
    o_refs = refs[:N_DIL]
    l_refs = refs[N_DIL:2 * N_DIL]
    out_ref = refs[2 * N_DIL]
    scratch = refs[2 * N_DIL + 1:]
    tm = out_ref.shape[0]
    vals = [o_refs[0][0], l_refs[0][0]]
    for g, (_, dil) in enumerate(DILATIONS[1:], start=1):
        for src, dst in ((o_refs[g], scratch[2 * g - 2]), (l_refs[g], scratch[2 * g - 1])):
            for h in range(H_D):
                for r in range(dil):
                    dst[h, pl.ds(r, tm // dil, stride=dil), :] = src[r, :, h * DH_D:(h + 1) * DH_D]
            vals.append(jnp.concatenate([dst[h] for h in range(H_D)], axis=1))
    outs, ls = vals[0::2], vals[1::2]
    mx = functools.reduce(jnp.maximum, ls)
    es = [jnp.exp2(l - mx) for l in ls]
    den = functools.reduce(lambda a, b: a + b, es)
    acc = None
    for e, o in zip(es, outs):
        term = (e / den) * o
        acc = term if acc is None else acc + term
    out_ref[...] = acc.astype(out_ref.dtype)


def band_mixture(outs, lses):
    _, m, c = outs[0].shape
    tm = _pick(m, (PREP_TM, 128, 64))
    specs = [pl.BlockSpec((dil, tm // dil, c), lambda i: (0, i, 0)) for _, dil in DILATIONS]
    return pl.pallas_call(
        _mix_body,
        out_shape=jax.ShapeDtypeStruct((m, c), BF16),
        grid=(m // tm,),
        in_specs=specs * 2,
        out_specs=pl.BlockSpec((tm, c), lambda i: (i, 0)),
        scratch_shapes=[pltpu.VMEM((H_D, tm, DH_D), F32)] * (2 * (N_DIL - 1)),
        compiler_params=_cparams(("parallel",)),
        name="band_mixture",
    )(*outs, *lses)


def kernel(x_prompt, x_sample, p_prompt, p_sample, rel_bias, norm_mix, w_in, a_q_norm, a_k_norm, a_lambda_q1, a_lambda_k1, a_lambda_q2, a_lambda_k2, a_out_norm, b_cq_norm, b_ckv_norm, b_w_uq, b_w_ukv, b_q_norm, b_k_norm, c_q_norm, c_k_norm, d_q_norm, d_k_norm, w_gate, w_branch, w_out, norm_ffn, w_ff1, w_ff2, norm_ple, w_ple_gate, w_ple_proj):
    depth = w_in.shape[0]
    d_model = x_prompt.shape[-1]
    groups = [(x_prompt.shape[0], x_prompt.shape[1]), (x_sample.shape[0], x_sample.shape[1])]
    row0s = [0, groups[0][0] * groups[0][1]]
    x = jnp.concatenate([x_prompt.reshape(-1, d_model), x_sample.reshape(-1, d_model)], axis=0)
    p_parts = [p_prompt.reshape(depth, -1, p_prompt.shape[-1]), p_sample.reshape(depth, -1, p_sample.shape[-1])]

    in_sizes = (512, 512, 512, 512, 512, 64, 512, 256, 256, 1536, 512, 512)
    offs = np.concatenate([[0], np.cumsum(in_sizes)])
    seg = lambda w, a, b: w[:, offs[a]:offs[b]]

    table_a = {}
    tables_d = {}
    for _, seq in groups:
        tk = _attn_tiles(seq)[1]
        if tk not in table_a:
            table_a[tk] = _diagonal_table(rel_bias[:, :H_A], tk)
        for _, dil in DILATIONS:
            tu = min(BAND_TU, seq // dil)
            if tu not in tables_d:
                tables_d[tu] = _band_table(rel_bias[:, H_A:], tu)

    two = lambda g: jnp.concatenate([g, g])
    pad_b = lambda g: jnp.pad(g, (0, DKP_B - DN_B - DR_B))
    h = rmsnorm(x, norm_mix[0])
    for i in range(depth):
        lam_init = 0.8 - 0.6 * math.exp(-0.3 * i)
        wi = w_in[i]
        w_in_p = jnp.concatenate([seg(wi, 0, 3), seg(wi, 6, 9), seg(wi, 9, 12), seg(wi, 3, 5)], axis=1).astype(BF16)
        w_kr_p = jnp.pad(seg(wi, 5, 6), ((0, 0), (0, LANES - DR_B))).astype(BF16)
        w_uq_p = jnp.pad(b_w_uq[i].reshape(Q_RANK_B, H_B, DN_B + DR_B),
                         ((0, 0), (0, 0), (0, DKP_B - DN_B - DR_B))).reshape(Q_RANK_B, H_B * DKP_B).astype(BF16)
        zeros = jnp.zeros((PAIR,), F32)
        gains256 = jnp.stack([two(two(a_q_norm[i])) * (DK_A ** -0.5 * LOG2E), two(two(a_k_norm[i])),
                              two(c_q_norm[i]) * (DH_C ** -0.5 * LOG2E), two(c_k_norm[i]),
                              two(d_q_norm[i]) * (DH_D ** -0.5 * LOG2E), two(d_k_norm[i]), zeros, zeros]).astype(F32)
        gains512 = jnp.stack([b_cq_norm[i], b_ckv_norm[i],
                              jnp.concatenate([pad_b(b_q_norm[i]) * ((DN_B + DR_B) ** -0.5 * LOG2E),
                                               pad_b(b_k_norm[i])])]
                             + [jnp.zeros((512,), F32)] * 5).astype(F32)
        lam = (jnp.exp(jnp.sum(a_lambda_q1[i].astype(F32) * a_lambda_k1[i].astype(F32)))
               - jnp.exp(jnp.sum(a_lambda_q2[i].astype(F32) * a_lambda_k2[i].astype(F32))) + lam_init)
        lam = lam.reshape(1).astype(F32)

        proj = matmul(h, w_in_p, out_dtype=F32)
        kr = matmul(h, w_kr_p, out_dtype=F32)
        (qa, ka, va, qb, kb, vb, qc, kc, vc, *qkv_d) = prepare_mixer_inputs(
            proj, kr, groups, gains256, gains512, w_uq_p, b_w_ukv[i].astype(BF16))

        branches = [[], [], [], []]
        for (nseq, seq), row0 in zip(groups, row0s):
            kw = dict(row0=row0, nseq=nseq, seq=seq)
            branches[0].append(diff_attention(qa, ka, va, table_a[_attn_tiles(seq)[1]], lam, a_out_norm[i],
                                              lam_init=lam_init, **kw))
            branches[1].append(dense_attention(qb, kb, vb, hq=H_B, hk=H_B, dk=DKP_B, dv=DV_B, **kw))
            branches[2].append(dense_attention(qc, kc, vc, hq=H_C, hk=KV_C, dk=DH_C, dv=DH_C, **kw))
            outs, lses = zip(*[band_attention(*qkv_d[3 * g:3 * g + 3], tables_d[min(BAND_TU, seq // dil)], g, **kw)
                               for g, (_, dil) in enumerate(DILATIONS)])
            branches[3].append(band_mixture(outs, lses))

        merged = gated_merge(h, w_gate[i].astype(BF16), w_branch[i].astype(BF16), branches, out_dtype=BF16)
        x, h2 = matmul(merged, w_out[i].astype(BF16), out_dtype=F32, res=x, norm_gain=norm_ffn[i])
        u = matmul(h2, w_ff1[i].astype(BF16), out_dtype=BF16, act="relu2")
        x = matmul(u, w_ff2[i].astype(BF16), out_dtype=F32, res=x, tn=512)
        h3 = rmsnorm(x, norm_ple[i])
        ple = dict(out_dtype=F32, res=x)
        ple_w = (h3, w_ple_gate[i].astype(BF16)[None], w_ple_proj[i].astype(BF16)[None])
        if i + 1 < depth:
            x, h = gated_merge(*ple_w, [[p[i] for p in p_parts]], norm_gain=norm_mix[i + 1], **ple)
        else:
            ys = [gated_merge(*ple_w, [[p[i]]], full_rows=True, row0=row0, nrows=nseq * seq, **ple)
                  for p, (nseq, seq), row0 in zip(p_parts, groups, row0s)]

    y_prompt, y_sample = ys[0].reshape(x_prompt.shape), ys[1].reshape(x_sample.shape)
    return (y_prompt, y_sample)
```

```python
import functools
import math

import jax
import jax.numpy as jnp
import numpy as np
from jax import lax
from jax.experimental import pallas as pl
from jax.experimental.pallas import tpu as pltpu

BF16 = jnp.bfloat16
F32 = jnp.float32

GRID_W = 64
H_A, DK_A, DV_A = 4, 64, 128
H_B, DN_B, DR_B, DV_B = 4, 128, 64, 128
Q_RANK_B = KV_RANK_B = 512
H_C, KV_C, DH_C = 4, 2, 128
H_D, DH_D = 4, 128
DILATIONS = ((128, 1), (512, 4), (2048, 16))
N_DIL = len(DILATIONS)
HALO = 64
assert all(w // (2 * d) == HALO for w, d in DILATIONS)
N_BUCKETS = 32
REL_MAX_DIST = 1024
ROPE_THETA = 10000.0
NORM_EPS = 1e-6
NEG_INF = -1e30
LOG2E = 1.4426950408889634
LANES = 128
DKP_B = 256
VMEM_LIMIT = 56 * 1024 * 1024

MM_TM, MM_TN, MM_TK = 1024, 1024, 2048
ROW_TM = 512
GATED_TN = 512
GATED_ROW_TM = 256
ATTN_TQ, ATTN_TK = 2048, 1024
PREP_TM = 256
BAND_TU = 256

COL_A, COL_C, COL_D, COL_BCQ, COL_BCKV, N_IN = 0, 1536, 2560, 5120, 5632, 6144


def _cparams(sem):
    return pltpu.CompilerParams(dimension_semantics=sem, vmem_limit_bytes=VMEM_LIMIT)


def _pick(n, prefs):
    for t in prefs:
        if n % t == 0:
            return t
    return n


def _rmsnorm_body(x_ref, g_ref, o_ref):
    x = x_ref[...]
    y = x * lax.rsqrt(jnp.mean(x * x, axis=-1, keepdims=True) + NORM_EPS)
    o_ref[...] = (y * g_ref[...]).astype(o_ref.dtype)


def rmsnorm(x, g, out_dtype=BF16):
    m, d = x.shape
    tm = _pick(m, (512, 256, 128, 64, 8))
    return pl.pallas_call(
        _rmsnorm_body,
        out_shape=jax.ShapeDtypeStruct((m, d), out_dtype),
        grid=(m // tm,),
        in_specs=[pl.BlockSpec((tm, d), lambda i: (i, 0)),
                  pl.BlockSpec((1, d), lambda i: (0, 0))],
        out_specs=pl.BlockSpec((tm, d), lambda i: (i, 0)),
        compiler_params=_cparams(("parallel",)),
        name="rmsnorm",
    )(x, g.reshape(1, d).astype(F32))


def _store_with_norm(acc, g_ref, o_ref, h_ref):
    o_ref[...] = acc
    inv = lax.rsqrt(jnp.mean(acc * acc, axis=-1, keepdims=True) + NORM_EPS)
    h_ref[...] = (acc * inv * g_ref[...]).astype(h_ref.dtype)


def _mm_body(*refs, act, has_res, has_norm, nk):
    a_ref, w_ref = refs[:2]
    n_in = 2 + has_res + has_norm
    r_ref = refs[2] if has_res else None
    g_ref = refs[n_in - 1] if has_norm else None
    o_ref = refs[n_in]
    h_ref = refs[n_in + 1] if has_norm else None
    scratch = refs[n_in + 1 + has_norm:]

    def finish(acc):
        if act == "relu2":
            acc = jnp.square(jnp.maximum(acc, 0.0))
        if has_res:
            acc = r_ref[...] + acc
        if has_norm:
            _store_with_norm(acc, g_ref, o_ref, h_ref)
        else:
            o_ref[...] = acc.astype(o_ref.dtype)

    part = jnp.dot(a_ref[...], w_ref[...], preferred_element_type=F32)
    if nk == 1:
        finish(part)
        return
    acc_ref, = scratch
    k = pl.program_id(2)

    @pl.when(k == 0)
    def _():
        acc_ref[...] = part

    @pl.when(k > 0)
    def _():
        acc_ref[...] += part

    @pl.when(k == nk - 1)
    def _():
        finish(acc_ref[...])


def matmul(a, w, *, out_dtype, act=None, res=None, norm_gain=None, tm=MM_TM, tn=MM_TN, tk=MM_TK):
    m, kdim = a.shape
    n = w.shape[1]
    has_norm = norm_gain is not None
    tm = _pick(m, ((ROW_TM if has_norm else tm), 512, 256, 128, 64, 8))
    tn = n if has_norm else _pick(n, (tn, 512, 256, 128))
    tk = _pick(kdim, (tk, 2048, 1024, 512))
    nk = kdim // tk
    in_specs = [pl.BlockSpec((tm, tk), lambda i, j, k: (i, k)),
                pl.BlockSpec((tk, tn), lambda i, j, k: (k, j))]
    args = [a, w]
    if res is not None:
        in_specs.append(pl.BlockSpec((tm, tn), lambda i, j, k: (i, j)))
        args.append(res)
    out_shape = jax.ShapeDtypeStruct((m, n), out_dtype)
    out_specs = pl.BlockSpec((tm, tn), lambda i, j, k: (i, j))
    if has_norm:
        in_specs.append(pl.BlockSpec((1, n), lambda i, j, k: (0, 0)))
        args.append(norm_gain.reshape(1, n).astype(F32))
        out_shape = (out_shape, jax.ShapeDtypeStruct((m, n), BF16))
        out_specs = (out_specs, pl.BlockSpec((tm, tn), lambda i, j, k: (i, j)))
    return pl.pallas_call(
        functools.partial(_mm_body, act=act, has_res=res is not None, has_norm=has_norm, nk=nk),
        out_shape=out_shape,
        grid=(m // tm, n // tn, nk),
        in_specs=in_specs,
        out_specs=out_specs,
        scratch_shapes=[pltpu.VMEM((tm, tn), F32)] if nk > 1 else [],
        compiler_params=_cparams(("parallel", "parallel", "arbitrary")),
        name="matmul",
    )(*args)


def _gated_body(*refs, nb, npart, split, has_res, has_norm, h_from_res):
    h_ref, wg_ref, wb_ref = refs[:3]
    br_refs = refs[3:3 + nb * npart]
    rest = list(refs[3 + nb * npart:])
    r_ref = rest.pop(0) if has_res else None
    g_ref = rest.pop(0) if has_norm else None
    o_ref = rest.pop(0)
    hn_ref = rest.pop(0) if has_norm else None
    if h_from_res:
        r = r_ref[...]
        h = (r * lax.rsqrt(jnp.mean(r * r, axis=-1, keepdims=True) + NORM_EPS) * h_ref[...]).astype(BF16)
    else:
        h = h_ref[...]
    first = pl.program_id(0) < split
    acc = None
    for b in range(nb):
        parts = br_refs[b * npart:(b + 1) * npart]
        x = parts[0][...]
        if npart == 2:
            x = jnp.where(first, x, parts[1][...])
        gate = jnp.dot(h, wg_ref[b], preferred_element_type=F32)
        val = jnp.dot(x.astype(BF16), wb_ref[b], preferred_element_type=F32)
        term = jax.nn.sigmoid(gate) * val
        acc = term if acc is None else acc + term
    if has_res:
        acc = r_ref[...] + acc
    if has_norm:
        _store_with_norm(acc, g_ref, o_ref, hn_ref)
    else:
        o_ref[...] = acc.astype(o_ref.dtype)


def gated_merge(h, wg, wb, branches, *, out_dtype, res=None, norm_gain=None, full_rows=False, row0=0, nrows=None):
    h_from_res = h.ndim == 1
    if h_from_res:
        h = h.reshape(1, -1).astype(F32)
        full_rows = True
    d = h.shape[1]
    m = nrows if nrows is not None else (res.shape[0] if h_from_res else h.shape[0])
    nb, _, n = wg.shape
    kb = wb.shape[1]
    has_norm = norm_gain is not None
    full_rows = full_rows or has_norm
    tm = _pick(math.gcd(m, row0), ((GATED_ROW_TM if full_rows else MM_TM), 256, 128, 64, 8))
    tn = n if full_rows else _pick(n, (GATED_TN, 128))
    i0 = row0 // tm
    npart = len(branches[0])
    rows0 = branches[0][0].shape[0]
    split = rows0 // tm if npart == 2 else m // tm
    assert npart in (1, 2) and rows0 % tm == 0
    in_specs = [pl.BlockSpec((1, d), lambda i, j: (0, 0)) if h_from_res
                else pl.BlockSpec((tm, d), lambda i, j: (i0 + i, 0)),
                pl.BlockSpec((nb, d, tn), lambda i, j: (0, 0, j)),
                pl.BlockSpec((nb, kb, tn), lambda i, j: (0, 0, j))]
    args = [h, wg, wb]
    for parts in branches:
        in_specs.append(pl.BlockSpec((tm, kb), lambda i, j: (jnp.minimum(i, split - 1), 0)))
        if npart == 2:
            in_specs.append(pl.BlockSpec((tm, kb), lambda i, j: (jnp.maximum(i - split, 0), 0)))
        args.extend(parts)
    if res is not None:
        in_specs.append(pl.BlockSpec((tm, tn), lambda i, j: (i0 + i, j)))
        args.append(res)
    out_shape = jax.ShapeDtypeStruct((m, n), out_dtype)
    out_specs = pl.BlockSpec((tm, tn), lambda i, j: (i, j))
    if has_norm:
        in_specs.append(pl.BlockSpec((1, n), lambda i, j: (0, 0)))
        args.append(norm_gain.reshape(1, n).astype(F32))
        out_shape = (out_shape, jax.ShapeDtypeStruct((m, n), BF16))
        out_specs = (out_specs, pl.BlockSpec((tm, tn), lambda i, j: (i, j)))
    return pl.pallas_call(
        functools.partial(_gated_body, nb=nb, npart=npart, split=split, has_res=res is not None,
                          has_norm=has_norm, h_from_res=h_from_res),
        out_shape=out_shape,
        grid=(m // tm, n // tn),
        in_specs=in_specs,
        out_specs=out_specs,
        compiler_params=_cparams(("parallel", "parallel")),
        name="gated_merge",
    )(*args)


PAIR = 2 * LANES


def _rope_rot(y, cos, sin_signed):
    first = (lax.broadcasted_iota(jnp.int32, (1, LANES), 1) % 64) < 32
    parts = []
    for c0 in range(0, y.shape[1], LANES):
        yb = y[:, c0:c0 + LANES]
        parts.append(jnp.where(first, pltpu.roll(yb, 96, 1), pltpu.roll(yb, 32, 1)))
    return y * cos + jnp.concatenate(parts, axis=1) * sin_signed


def _group_mean_matrix(width):
    shift = width.bit_length() - 1
    r = lax.broadcasted_iota(jnp.int32, (PAIR, PAIR), 0) >> shift
    c = lax.broadcasted_iota(jnp.int32, (PAIR, PAIR), 1) >> shift
    return jnp.where(r == c, 1.0 / width, 0.0).astype(BF16)


def _group_meansq(x, mat):
    return jnp.dot((x * x).astype(BF16), mat, preferred_element_type=F32)


def _inv_rms_of_mean(ms, ratio=1.0):
    return lax.rsqrt((ms if ratio == 1.0 else ms * ratio) + NORM_EPS)


def _prep_body(proj_ref, kr_ref, cosb_ref, sinb_ref, cosc_ref, sinc_ref, g256_ref, g512_ref, wuq_ref, wukv_ref,
               qa_ref, ka_ref, va_ref, qb_ref, kb_ref, vb_ref, qc_ref, kc_ref, vc_ref,
               qd0_ref, kd0_ref, vd0_ref, qd1_ref, kd1_ref, vd1_ref, qd2_ref, kd2_ref, vd2_ref,
               sq1, sq2, sk, sv):
    tm = proj_ref.shape[0]
    mean64, mean128, mean256 = _group_mean_matrix(DK_A), _group_mean_matrix(LANES), _group_mean_matrix(PAIR)
    blk = lambda c0, j: slice(c0 + j * LANES, c0 + (j + 1) * LANES)
    pair = lambda c0, j: slice(c0 + j * PAIR, c0 + (j + 1) * PAIR)
    gain = lambda r: g256_ref[r:r + 1, :]
    twice = lambda t: jnp.concatenate([t, t], axis=1)

    def normed(c0, j, mat, grow):
        x = proj_ref[:, pair(c0, j)]
        return x * _inv_rms_of_mean(_group_meansq(x, mat)) * gain(grow)

    for base, grow, dst in ((COL_A, 0, qa_ref), (COL_A + 512, 1, ka_ref)):
        for j in range(H_A // 2):
            dst[:, pair(0, j)] = normed(base, j, mean64, grow).astype(BF16)
    va_ref[...] = proj_ref[:, COL_A + 1024:COL_A + 1536].astype(BF16)

    cosc, sinc = twice(cosc_ref[...]), twice(sinc_ref[...])
    for base, nh, grow, dst in ((COL_C, H_C, 2, qc_ref), (COL_C + 512, KV_C, 3, kc_ref)):
        for j in range(nh // 2):
            dst[:, pair(0, j)] = _rope_rot(normed(base, j, mean128, grow), cosc, sinc).astype(BF16)
    vc_ref[...] = proj_ref[:, COL_C + 768:COL_C + 1024].astype(BF16)

    for j in range(H_D // 2):
        qd0_ref[0, :, pair(0, j)] = normed(COL_D, j, mean128, 4).astype(BF16)
        yk = normed(COL_D + 1536, j, mean128, 5)
        kd0_ref[0, :, pair(0, j)] = yk.astype(BF16)
        for scr, y in ((sq1, normed(COL_D + 512, j, mean128, 4)),
                       (sq2, normed(COL_D + 1024, j, mean128, 4)), (sk, yk)):
            scr[2 * j] = y[:, :LANES]
            scr[2 * j + 1] = y[:, LANES:]
    for h in range(H_D):
        sv[h] = proj_ref[:, blk(COL_D + 2048, h)]
    vd0_ref[0] = proj_ref[:, COL_D + 2048:COL_D + 2560].astype(BF16)
    for (_, dil), dsts in zip(DILATIONS[1:], (((sq1, qd1_ref), (sk, kd1_ref), (sv, vd1_ref)),
                                              ((sq2, qd2_ref), (sk, kd2_ref), (sv, vd2_ref)))):
        for src, dst in dsts:
            for h in range(H_D):
                for r in range(dil):
                    dst[r, :, blk(0, h)] = src[h, pl.ds(r, tm // dil, stride=dil), :].astype(BF16)

    def latent(c0, grow):
        x = proj_ref[:, c0:c0 + 2 * PAIR]
        ms = _group_meansq(x[:, :PAIR], mean256) + _group_meansq(x[:, PAIR:], mean256)
        return (x * twice(_inv_rms_of_mean(ms, 0.5)) * g512_ref[grow:grow + 1, :]).astype(BF16)

    qraw = jnp.dot(latent(COL_BCQ, 0), wuq_ref[...], preferred_element_type=F32)
    kvraw = jnp.dot(latent(COL_BCKV, 1), wukv_ref[...], preferred_element_type=F32)
    kr = kr_ref[...]
    cosb, sinb = cosb_ref[...], sinb_ref[...]
    gq, gk = g512_ref[2:3, 0:PAIR], g512_ref[2:3, PAIR:2 * PAIR]
    for h in range(H_B):
        xq = qraw[:, pair(0, h)]
        xk = jnp.concatenate([kvraw[:, blk(0, 2 * h)], kr], axis=1)
        for x, g, dst in ((xq, gq, qb_ref), (xk, gk, kb_ref)):
            y = x * _inv_rms_of_mean(_group_meansq(x, mean256), PAIR / (DN_B + DR_B)) * g
            dst[:, blk(0, 2 * h)] = y[:, :LANES].astype(BF16)
            dst[:, blk(0, 2 * h + 1)] = _rope_rot(y[:, LANES:], cosb, sinb).astype(BF16)
        vb_ref[:, blk(0, h)] = kvraw[:, blk(0, 2 * h + 1)].astype(BF16)


def _rope_lanes(pos):
    half = DR_B // 2
    inv = jnp.power(ROPE_THETA, -2.0 * jnp.arange(half, dtype=F32) / DR_B)
    ang = pos[:, None] * inv[None, :]
    cos, sin = jnp.cos(ang), jnp.sin(ang)
    return jnp.concatenate([cos, cos], axis=1), jnp.concatenate([-sin, sin], axis=1)


def prepare_mixer_inputs(proj, kr, groups, gains256, gains512, w_uq_p, w_ukv):
    m = proj.shape[0]
    tm = _pick(math.gcd(*[s for _, s in groups]), (PREP_TM, 128, 64, 8))
    smax = max(s for _, s in groups)
    pos = jnp.arange(smax, dtype=jnp.int32)
    cb, sb = _rope_lanes(pos.astype(F32))
    cosb = jnp.concatenate([cb, jnp.ones((smax, 64), F32)], axis=1)
    sinb = jnp.concatenate([sb, jnp.zeros((smax, 64), F32)], axis=1)
    cr, sr = _rope_lanes((pos // GRID_W).astype(F32))
    cc, sc = _rope_lanes((pos % GRID_W).astype(F32))
    cosc = jnp.concatenate([cr, cc], axis=1)
    sinc = jnp.concatenate([sr, sc], axis=1)
    (b0, s0), (_, s1) = groups
    t0, q1 = b0 * s0 // tm, s1 // tm

    def pos_block(i):
        return (jnp.where(i < t0, i % (s0 // tm), (i - t0) % q1), 0)

    row = lambda c: pl.BlockSpec((tm, c), lambda i: (i, 0))
    full = lambda a: pl.BlockSpec(a.shape, lambda i: (0,) * a.ndim)
    tab = pl.BlockSpec((tm, LANES), pos_block)
    wd = H_D * DH_D
    widths = (512, 512, 512, H_B * DKP_B, H_B * DKP_B, H_B * DV_B, 512, 256, 256)
    out_shape = [jax.ShapeDtypeStruct((m, c), BF16) for c in widths]
    out_specs = [row(c) for c in widths]
    for _, dil in DILATIONS:
        out_shape += [jax.ShapeDtypeStruct((dil, m // dil, wd), BF16)] * 3
        out_specs += [pl.BlockSpec((dil, tm // dil, wd), lambda i: (0, i, 0))] * 3
    return pl.pallas_call(
        _prep_body,
        out_shape=tuple(out_shape),
        grid=(m // tm,),
        in_specs=[row(N_IN), row(LANES), tab, tab, tab, tab, full(gains256), full(gains512),
                  full(w_uq_p), full(w_ukv)],
        out_specs=tuple(out_specs),
        scratch_shapes=[pltpu.VMEM((H_D, tm, DH_D), F32)] * 4,
        compiler_params=_cparams(("parallel",)),
        name="prepare_mixer_inputs",
    )(proj, kr, cosb, sinb, cosc, sinc, gains256, gains512, w_uq_p, w_ukv)


def _softmax_step_t(k, q, v, m_ref, l_ref, acc_ref, bias=None):
    st = _kq(k, q)
    if bias is not None:
        st = st + bias
    m_prev = m_ref[...]
    m_new = jnp.maximum(m_prev, jnp.max(st, axis=0, keepdims=True))
    alpha = jnp.exp2(m_prev - m_new)
    pt = jnp.exp2(st - m_new)
    l_ref[...] = alpha * l_ref[...] + jnp.sum(pt, axis=0, keepdims=True)
    pv = lax.dot_general(v, pt.astype(BF16), (((0,), (0,)), ((), ())), preferred_element_type=F32)
    acc_ref[...] = alpha * acc_ref[...] + pv
    m_ref[...] = m_new


def _init_stats(m_ref, l_ref, acc_ref):
    m_ref[...] = jnp.full(m_ref.shape, -jnp.inf, F32)
    l_ref[...] = jnp.zeros(l_ref.shape, F32)
    acc_ref[...] = jnp.zeros(acc_ref.shape, F32)


def _kq(k, q):
    return lax.dot_general(k, q, (((1,), (1,)), ((), ())), preferred_element_type=F32)


def _attn_body(q_ref, k_ref, v_ref, o_ref, m_ref, l_ref, acc_ref, *, tk, nk):
    _init_stats(m_ref, l_ref, acc_ref)
    q = q_ref[...]

    def step(j, carry):
        off = pl.multiple_of(j * tk, tk)
        _softmax_step_t(k_ref[pl.ds(off, tk), :], q, v_ref[pl.ds(off, tk), :], m_ref, l_ref, acc_ref)
        return carry

    lax.fori_loop(0, nk, step, 0)
    o_ref[...] = jnp.transpose(acc_ref[...] / l_ref[...]).astype(o_ref.dtype)


def _attn_scratch(tq, dv):
    return [pltpu.VMEM((1, tq), F32), pltpu.VMEM((1, tq), F32), pltpu.VMEM((dv, tq), F32)]


def _attn_tiles(seq):
    return _pick(seq, (ATTN_TQ, 1024, 512, 256, 128)), _pick(seq, (ATTN_TK, 256, 128))


def dense_attention(q, k, v, *, row0, nseq, seq, hq, hk, dk, dv):
    tq, tk = _attn_tiles(seq)
    nq = seq // tq
    group = hq // hk
    qb0 = row0 // tq
    sb0 = row0 // seq
    return pl.pallas_call(
        functools.partial(_attn_body, tk=tk, nk=seq // tk),
        out_shape=jax.ShapeDtypeStruct((nseq * seq, hq * dv), BF16),
        grid=(nseq, hq, nq),
        in_specs=[pl.BlockSpec((tq, dk), lambda b, h, i: (qb0 + b * nq + i, h)),
                  pl.BlockSpec((seq, dk), lambda b, h, i: (sb0 + b, h // group)),
                  pl.BlockSpec((seq, dv), lambda b, h, i: (sb0 + b, h // group))],
        out_specs=pl.BlockSpec((tq, dv), lambda b, h, i: (b * nq + i, h)),
        scratch_shapes=_attn_scratch(tq, dv),
        compiler_params=_cparams(("parallel", "parallel", "parallel")),
        name="dense_attention",
    )(q, k, v)


BIAS_FAR = 1152


def _diff_body(lam_ref, q_ref, k_ref, v_ref, bias_ref, g_ref, o_ref,
               m1, l1, a1, m2, l2, a2, *, tq, tk, nk, lo, out_scale):
    _init_stats(m1, l1, a1)
    _init_stats(m2, l2, a2)
    q = q_ref[...]
    lane = lax.broadcasted_iota(jnp.int32, q.shape, 1)
    zero = jnp.zeros_like(q)
    q1 = jnp.where(lane < DK_A, q, zero)
    q2 = jnp.where(lane >= DK_A, q, zero)
    q0 = pl.program_id(2) * tq

    def step(j, carry):
        off = pl.multiple_of(j * tk, tk)
        k = k_ref[pl.ds(off, tk), :]
        v = v_ref[pl.ds(off, tk), :]
        cols = []
        for c0 in range(0, tq, LANES):
            r0 = jnp.clip(off - q0 - c0, lo, BIAS_FAR) - lo
            cols.append(bias_ref[pl.ds(pl.multiple_of(r0, LANES), tk), :])
        bias = jnp.concatenate(cols, axis=1)
        _softmax_step_t(k, q1, v, m1, l1, a1, bias)
        _softmax_step_t(k, q2, v, m2, l2, a2, bias)
        return carry

    lax.fori_loop(0, nk, step, 0)
    o = a1[...] / l1[...] - lam_ref[0] * (a2[...] / l2[...])
    y = o * lax.rsqrt(jnp.mean(o * o, axis=0, keepdims=True) + NORM_EPS)
    o_ref[...] = jnp.transpose((y * g_ref[...]) * out_scale).astype(o_ref.dtype)


def _rel_bucket(rel):
    half = N_BUCKETS // 2
    exact = half // 2
    n = jnp.abs(rel)
    nf = jnp.maximum(n, 1).astype(F32)
    large = exact + (jnp.log(nf / exact) / math.log(REL_MAX_DIST / exact) * (half - exact)).astype(jnp.int32)
    large = jnp.minimum(large, half - 1)
    return jnp.where(rel > 0, half, 0) + jnp.where(n < exact, n, large)


def _skew(vec, nrows):
    p = vec.shape[-1]
    flat = jnp.tile(vec, (1,) * (vec.ndim - 1) + (nrows,))[..., :nrows * (p - 1)]
    return flat.reshape(vec.shape[:-1] + (nrows, p - 1))


def _diff_bias_geometry(tk):
    lo = -(REL_MAX_DIST + tk)
    return lo, BIAS_FAR - lo + tk


def _diagonal_table(tbl, tk):
    lo, nrows = _diff_bias_geometry(tk)
    rel = jnp.arange(lo - (LANES - 1), lo + nrows + 1, dtype=jnp.int32)
    vec = jnp.transpose(tbl[_rel_bucket(rel)]).astype(F32) * LOG2E
    diag = _skew(vec, LANES)[:, :, LANES - 1:LANES - 1 + nrows]
    return jnp.transpose(diag, (0, 2, 1))


def diff_attention(q, k, v, table, lam, out_g, *, row0, nseq, seq, lam_init):
    tq, tk = _attn_tiles(seq)
    nq = seq // tq
    qb0 = row0 // tq
    sb0 = row0 // seq
    lo, nrows = _diff_bias_geometry(tk)
    assert table.shape == (H_A, nrows, LANES)
    return pl.pallas_call(
        functools.partial(_diff_body, tq=tq, tk=tk, nk=seq // tk, lo=lo, out_scale=1.0 - lam_init),
        out_shape=jax.ShapeDtypeStruct((nseq * seq, H_A * DV_A), BF16),
        grid=(nseq, H_A, nq),
        in_specs=[pl.BlockSpec(memory_space=pltpu.SMEM),
                  pl.BlockSpec((tq, 2 * DK_A), lambda b, h, i: (qb0 + b * nq + i, h)),
                  pl.BlockSpec((seq, 2 * DK_A), lambda b, h, i: (sb0 + b, h)),
                  pl.BlockSpec((seq, DV_A), lambda b, h, i: (sb0 + b, h)),
                  pl.BlockSpec((None, nrows, LANES), lambda b, h, i: (h, 0, 0)),
                  pl.BlockSpec((DV_A, 1), lambda b, h, i: (0, 0))],
        out_specs=pl.BlockSpec((tq, DV_A), lambda b, h, i: (b * nq + i, h)),
        scratch_shapes=_attn_scratch(tq, DV_A) + _attn_scratch(tq, DV_A),
        compiler_params=_cparams(("parallel", "parallel", "parallel")),
        name="diff_attention",
    )(lam, q, k, v, table, out_g.reshape(DV_A, 1).astype(F32))


def _band_body(q_ref, kp_ref, kc_ref, kn_ref, vp_ref, vc_ref, vn_ref, bias_ref, o_ref, lse_ref,
               *, tu, nb):
    n = pl.program_id(2)
    col = lax.broadcasted_iota(jnp.int32, (1, tu + 2 * HALO), 1)
    outside = ((col < HALO) & (n == 0)) | ((col >= tu + HALO) & (n == nb - 1))
    for h in range(H_D):
        hs = slice(h * DH_D, (h + 1) * DH_D)
        k = jnp.concatenate([kp_ref[:, hs], kc_ref[:, hs], kn_ref[:, hs]], axis=0)
        v = jnp.concatenate([vp_ref[:, hs], vc_ref[:, hs], vn_ref[:, hs]], axis=0)
        s = _kq(q_ref[:, hs], k) + bias_ref[h]
        s = jnp.where(outside, NEG_INF, s)
        m = jnp.max(s, axis=1, keepdims=True)
        p = jnp.exp2(s - m)
        l = jnp.sum(p, axis=1, keepdims=True)
        o = jnp.dot(p.astype(BF16), v, preferred_element_type=F32) / l
        o_ref[:, hs] = o.astype(o_ref.dtype)
        lse_ref[:, hs] = jnp.broadcast_to(m + jnp.log2(l), (tu, DH_D))


def _band_table(bias_d, tu):
    p = 2 * tu + 2 * HALO + 1
    x = jnp.arange(p, dtype=jnp.int32)
    vecs = []
    for g, (_, dil) in enumerate(DILATIONS):
        tbl = bias_d[:, g * H_D:(g + 1) * H_D]
        band = jnp.transpose(tbl[_rel_bucket((x - HALO) * dil)]).astype(F32) * LOG2E
        vecs.append(jnp.where((x <= 2 * HALO)[None], band, NEG_INF))
    return _skew(jnp.stack(vecs), tu)[..., :tu + 2 * HALO]


def band_attention(q, k, v, table, g, *, row0, nseq, seq):
    dil, rows, width = q.shape
    length = seq // dil
    tu = table.shape[2]
    nb = length // tu
    u0 = row0 // dil
    hpt = tu // HALO
    last_halo = rows // HALO - 1

    def tile(b, n):
        return u0 // tu + b * nb + n

    cur = pl.BlockSpec((None, tu, width), lambda b, r, n: (r, tile(b, n), 0))
    prev = pl.BlockSpec((None, HALO, width), lambda b, r, n: (r, jnp.maximum(tile(b, n) * hpt - 1, 0), 0))
    nxt = pl.BlockSpec((None, HALO, width),
                       lambda b, r, n: (r, jnp.minimum((tile(b, n) + 1) * hpt, last_halo), 0))
    out_rows = nseq * length
    out_spec = pl.BlockSpec((None, tu, width), lambda b, r, n: (r, b * nb + n, 0))
    return pl.pallas_call(
        functools.partial(_band_body, tu=tu, nb=nb),
        out_shape=(jax.ShapeDtypeStruct((dil, out_rows, width), F32),
                   jax.ShapeDtypeStruct((dil, out_rows, width), F32)),
        grid=(nseq, dil, nb),
        in_specs=[cur, prev, cur, nxt, prev, cur, nxt,
                  pl.BlockSpec((None, H_D, tu, tu + 2 * HALO), lambda b, r, n: (g, 0, 0, 0))],
        out_specs=(out_spec, out_spec),
        compiler_params=_cparams(("parallel", "parallel", "parallel")),
        name="band_attention",
    )(q, k, k, k, v, v, v, table)


def _mix_body(*refs):
    o_refs = refs[:N_DIL]
    l_refs = refs[N_DIL:2 * N_DIL]
    out_ref = refs[2 * N_DIL]
    scratch = refs[2 * N_DIL + 1:]
    tm = out_ref.shape[0]
    vals = [o_refs[0][0], l_refs[0][0]]
    for g, (_, dil) in enumerate(DILATIONS[1:], start=1):
        for src, dst in ((o_refs[g], scratch[2 * g - 2]), (l_refs[g], scratch[2 * g - 1])):
            for h in range(H_D):
                for r in range(dil):
                    dst[h, pl.ds(r, tm // dil, stride=dil), :] = src[r, :, h * DH_D:(h + 1) * DH_D]
            vals.append(jnp.concatenate([dst[h] for h in range(H_D)], axis=1))
    outs, ls = vals[0::2], vals[1::2]
    mx = functools.reduce(jnp.maximum, ls)
    es = [jnp.exp2(l - mx) for l in ls]
    den = functools.reduce(lambda a, b: a + b, es)
    acc = None
    for e, o in zip(es, outs):
        term = (e / den) * o
        acc = term if acc is None else acc + term
    out_ref[...] = acc.astype(out_ref.dtype)


def band_mixture(outs, lses):
    _, m, c = outs[0].shape
    tm = _pick(m, (PREP_TM, 128, 64))
    specs = [pl.BlockSpec((dil, tm // dil, c), lambda i: (0, i, 0)) for _, dil in DILATIONS]
    return pl.pallas_call(
        _mix_body,
        out_shape=jax.ShapeDtypeStruct((m, c), BF16),
        grid=(m // tm,),
        in_specs=specs * 2,
        out_specs=pl.BlockSpec((tm, c), lambda i: (i, 0)),
        scratch_shapes=[pltpu.VMEM((H_D, tm, DH_D), F32)] * (2 * (N_DIL - 1)),
        compiler_params=_cparams(("parallel",)),
        name="band_mixture",
    )(*outs, *lses)


def kernel(x_prompt, x_sample, p_prompt, p_sample, rel_bias, norm_mix, w_in, a_q_norm, a_k_norm, a_lambda_q1, a_lambda_k1, a_lambda_q2, a_lambda_k2, a_out_norm, b_cq_norm, b_ckv_norm, b_w_uq, b_w_ukv, b_q_norm, b_k_norm, c_q_norm, c_k_norm, d_q_norm, d_k_norm, w_gate, w_branch, w_out, norm_ffn, w_ff1, w_ff2, norm_ple, w_ple_gate, w_ple_proj):
    depth = w_in.shape[0]
    d_model = x_prompt.shape[-1]
    groups = [(x_prompt.shape[0], x_prompt.shape[1]), (x_sample.shape[0], x_sample.shape[1])]
    row0s = [0, groups[0][0] * groups[0][1]]
    x = jnp.concatenate([x_prompt.reshape(-1, d_model), x_sample.reshape(-1, d_model)], axis=0)
    p_parts = [p_prompt.reshape(depth, -1, p_prompt.shape[-1]), p_sample.reshape(depth, -1, p_sample.shape[-1])]

    in_sizes = (512, 512, 512, 512, 512, 64, 512, 256, 256, 1536, 512, 512)
    offs = np.concatenate([[0], np.cumsum(in_sizes)])
    seg = lambda w, a, b: w[:, offs[a]:offs[b]]

    table_a = {}
    tables_d = {}
    for _, seq in groups:
        tk = _attn_tiles(seq)[1]
        if tk not in table_a:
            table_a[tk] = _diagonal_table(rel_bias[:, :H_A], tk)
        for _, dil in DILATIONS:
            tu = min(BAND_TU, seq // dil)
            if tu not in tables_d:
                tables_d[tu] = _band_table(rel_bias[:, H_A:], tu)

    two = lambda g: jnp.concatenate([g, g])
    pad_b = lambda g: jnp.pad(g, (0, DKP_B - DN_B - DR_B))
    h = rmsnorm(x, norm_mix[0])
    for i in range(depth):
        lam_init = 0.8 - 0.6 * math.exp(-0.3 * i)
        wi = w_in[i]
        w_in_p = jnp.concatenate([seg(wi, 0, 3), seg(wi, 6, 9), seg(wi, 9, 12), seg(wi, 3, 5)], axis=1).astype(BF16)
        w_kr_p = jnp.pad(seg(wi, 5, 6), ((0, 0), (0, LANES - DR_B))).astype(BF16)
        w_uq_p = jnp.pad(b_w_uq[i].reshape(Q_RANK_B, H_B, DN_B + DR_B),
                         ((0, 0), (0, 0), (0, DKP_B - DN_B - DR_B))).reshape(Q_RANK_B, H_B * DKP_B).astype(BF16)
        zeros = jnp.zeros((PAIR,), F32)
        gains256 = jnp.stack([two(two(a_q_norm[i])) * (DK_A ** -0.5 * LOG2E), two(two(a_k_norm[i])),
                              two(c_q_norm[i]) * (DH_C ** -0.5 * LOG2E), two(c_k_norm[i]),
                              two(d_q_norm[i]) * (DH_D ** -0.5 * LOG2E), two(d_k_norm[i]), zeros, zeros]).astype(F32)
        gains512 = jnp.stack([b_cq_norm[i], b_ckv_norm[i],
                              jnp.concatenate([pad_b(b_q_norm[i]) * ((DN_B + DR_B) ** -0.5 * LOG2E),
                                               pad_b(b_k_norm[i])])]
                             + [jnp.zeros((512,), F32)] * 5).astype(F32)
        lam = (jnp.exp(jnp.sum(a_lambda_q1[i].astype(F32) * a_lambda_k1[i].astype(F32)))
               - jnp.exp(jnp.sum(a_lambda_q2[i].astype(F32) * a_lambda_k2[i].astype(F32))) + lam_init)
        lam = lam.reshape(1).astype(F32)

        proj = matmul(h, w_in_p, out_dtype=F32)
        kr = matmul(h, w_kr_p, out_dtype=F32)
        (qa, ka, va, qb, kb, vb, qc, kc, vc, *qkv_d) = prepare_mixer_inputs(
            proj, kr, groups, gains256, gains512, w_uq_p, b_w_ukv[i].astype(BF16))

        branches = [[], [], [], []]
        for (nseq, seq), row0 in zip(groups, row0s):
            kw = dict(row0=row0, nseq=nseq, seq=seq)
            branches[0].append(diff_attention(qa, ka, va, table_a[_attn_tiles(seq)[1]], lam, a_out_norm[i],
                                              lam_init=lam_init, **kw))
            branches[1].append(dense_attention(qb, kb, vb, hq=H_B, hk=H_B, dk=DKP_B, dv=DV_B, **kw))
            branches[2].append(dense_attention(qc, kc, vc, hq=H_C, hk=KV_C, dk=DH_C, dv=DH_C, **kw))
            outs, lses = zip(*[band_attention(*qkv_d[3 * g:3 * g + 3], tables_d[min(BAND_TU, seq // dil)], g, **kw)
                               for g, (_, dil) in enumerate(DILATIONS)])
            branches[3].append(band_mixture(outs, lses))

        merged = gated_merge(h, w_gate[i].astype(BF16), w_branch[i].astype(BF16), branches, out_dtype=BF16)
        x, h2 = matmul(merged, w_out[i].astype(BF16), out_dtype=F32, res=x, norm_gain=norm_ffn[i])
        u = matmul(h2, w_ff1[i].astype(BF16), out_dtype=BF16, act="relu2")
        x = matmul(u, w_ff2[i].astype(BF16), out_dtype=F32, res=x, tn=256, tk=u.shape[1])
        ple = dict(out_dtype=F32, res=x)
        ple_w = (norm_ple[i], w_ple_gate[i].astype(BF16)[None], w_ple_proj[i].astype(BF16)[None])
        if i + 1 < depth:
            x, h = gated_merge(*ple_w, [[p[i] for p in p_parts]], norm_gain=norm_mix[i + 1], **ple)
        else:
            ys = [gated_merge(*ple_w, [[p[i]]], full_rows=True, row0=row0, nrows=nseq * seq, **ple)
                  for p, (nseq, seq), row0 in zip(p_parts, groups, row0s)]

    y_prompt, y_sample = ys[0].reshape(x_prompt.shape), ys[1].reshape(x_sample.shape)
    return (y_prompt, y_sample)
```

```python
import functools
import math

import jax
import jax.numpy as jnp
import numpy as np
from jax import lax
from jax.experimental import pallas as pl
from jax.experimental.pallas import tpu as pltpu

BF16 = jnp.bfloat16
F32 = jnp.float32

GRID_W = 64
H_A, DK_A, DV_A = 4, 64, 128
H_B, DN_B, DR_B, DV_B = 4, 128, 64, 128
Q_RANK_B = KV_RANK_B = 512
H_C, KV_C, DH_C = 4, 2, 128
H_D, DH_D = 4, 128
DILATIONS = ((128, 1), (512, 4), (2048, 16))
N_DIL = len(DILATIONS)
HALO = 64
assert all(w // (2 * d) == HALO for w, d in DILATIONS)
N_BUCKETS = 32
REL_MAX_DIST = 1024
ROPE_THETA = 10000.0
NORM_EPS = 1e-6
NEG_INF = -1e30
LOG2E = 1.4426950408889634
LANES = 128
DKP_B = 256
VMEM_LIMIT = 56 * 1024 * 1024

MM_TM, MM_TN, MM_TK = 1024, 1024, 2048
ROW_TM = 512
GATED_TN = 256
GATED_ROW_TM = 256
ATTN_TQ, ATTN_TK = 2048, 1024
PREP_TM = 256
BAND_TU = 256

COL_A, COL_C, COL_D, COL_BCQ, COL_BCKV, N_IN = 0, 1536, 2560, 5120, 5632, 6144


def _cparams(sem):
    return pltpu.CompilerParams(dimension_semantics=sem, vmem_limit_bytes=VMEM_LIMIT)


def _pick(n, prefs):
    for t in prefs:
        if n % t == 0:
            return t
    return n


def _rmsnorm_body(x_ref, g_ref, o_ref):
    x = x_ref[...]
    y = x * lax.rsqrt(jnp.mean(x * x, axis=-1, keepdims=True) + NORM_EPS)
    o_ref[...] = (y * g_ref[...]).astype(o_ref.dtype)


def rmsnorm(x, g, out_dtype=BF16):
    m, d = x.shape
    tm = _pick(m, (512, 256, 128, 64, 8))
    return pl.pallas_call(
        _rmsnorm_body,
        out_shape=jax.ShapeDtypeStruct((m, d), out_dtype),
        grid=(m // tm,),
        in_specs=[pl.BlockSpec((tm, d), lambda i: (i, 0)),
                  pl.BlockSpec((1, d), lambda i: (0, 0))],
        out_specs=pl.BlockSpec((tm, d), lambda i: (i, 0)),
        compiler_params=_cparams(("parallel",)),
        name="rmsnorm",
    )(x, g.reshape(1, d).astype(F32))


def _store_with_norm(acc, g_ref, o_ref, h_ref):
    o_ref[...] = acc
    inv = lax.rsqrt(jnp.mean(acc * acc, axis=-1, keepdims=True) + NORM_EPS)
    h_ref[...] = (acc * inv * g_ref[...]).astype(h_ref.dtype)


def _mm_body(*refs, act, has_res, has_norm, nk):
    a_ref, w_ref = refs[:2]
    n_in = 2 + has_res + has_norm
    r_ref = refs[2] if has_res else None
    g_ref = refs[n_in - 1] if has_norm else None
    o_ref = refs[n_in]
    h_ref = refs[n_in + 1] if has_norm else None
    scratch = refs[n_in + 1 + has_norm:]

    def finish(acc):
        if act == "relu2":
            acc = jnp.square(jnp.maximum(acc, 0.0))
        if has_res:
            acc = r_ref[...] + acc
        if has_norm:
            _store_with_norm(acc, g_ref, o_ref, h_ref)
        else:
            o_ref[...] = acc.astype(o_ref.dtype)

    part = jnp.dot(a_ref[...], w_ref[...], preferred_element_type=F32)
    if nk == 1:
        finish(part)
        return
    acc_ref, = scratch
    k = pl.program_id(2)

    @pl.when(k == 0)
    def _():
        acc_ref[...] = part

    @pl.when(k > 0)
    def _():
        acc_ref[...] += part

    @pl.when(k == nk - 1)
    def _():
        finish(acc_ref[...])


def matmul(a, w, *, out_dtype, act=None, res=None, norm_gain=None, tm=MM_TM, tn=MM_TN, tk=MM_TK):
    m, kdim = a.shape
    n = w.shape[1]
    has_norm = norm_gain is not None
    tm = _pick(m, ((ROW_TM if has_norm else tm), 512, 256, 128, 64, 8))
    tn = n if has_norm else _pick(n, (tn, 512, 256, 128))
    tk = _pick(kdim, (tk, 2048, 1024, 512))
    nk = kdim // tk
    in_specs = [pl.BlockSpec((tm, tk), lambda i, j, k: (i, k)),
                pl.BlockSpec((tk, tn), lambda i, j, k: (k, j))]
    args = [a, w]
    if res is not None:
        in_specs.append(pl.BlockSpec((tm, tn), lambda i, j, k: (i, j)))
        args.append(res)
    out_shape = jax.ShapeDtypeStruct((m, n), out_dtype)
    out_specs = pl.BlockSpec((tm, tn), lambda i, j, k: (i, j))
    if has_norm:
        in_specs.append(pl.BlockSpec((1, n), lambda i, j, k: (0, 0)))
        args.append(norm_gain.reshape(1, n).astype(F32))
        out_shape = (out_shape, jax.ShapeDtypeStruct((m, n), BF16))
        out_specs = (out_specs, pl.BlockSpec((tm, tn), lambda i, j, k: (i, j)))
    return pl.pallas_call(
        functools.partial(_mm_body, act=act, has_res=res is not None, has_norm=has_norm, nk=nk),
        out_shape=out_shape,
        grid=(m // tm, n // tn, nk),
        in_specs=in_specs,
        out_specs=out_specs,
        scratch_shapes=[pltpu.VMEM((tm, tn), F32)] if nk > 1 else [],
        compiler_params=_cparams(("parallel", "parallel", "arbitrary")),
        name="matmul",
    )(*args)


def _gated_body(*refs, nb, npart, split, has_res, has_norm, h_from_res):
    h_ref, wg_ref, wb_ref = refs[:3]
    br_refs = refs[3:3 + nb * npart]
    rest = list(refs[3 + nb * npart:])
    r_ref = rest.pop(0) if has_res else None
    g_ref = rest.pop(0) if has_norm else None
    o_ref = rest.pop(0)
    hn_ref = rest.pop(0) if has_norm else None
    if h_from_res:
        r = r_ref[...]
        h = (r * lax.rsqrt(jnp.mean(r * r, axis=-1, keepdims=True) + NORM_EPS) * h_ref[...]).astype(BF16)
    else:
        h = h_ref[...]
    first = pl.program_id(0) < split
    acc = None
    for b in range(nb):
        parts = br_refs[b * npart:(b + 1) * npart]
        x = parts[0][...]
        if npart == 2:
            x = jnp.where(first, x, parts[1][...])
        gate = jnp.dot(h, wg_ref[b], preferred_element_type=F32)
        val = jnp.dot(x.astype(BF16), wb_ref[b], preferred_element_type=F32)
        term = jax.nn.sigmoid(gate) * val
        acc = term if acc is None else acc + term
    if has_res:
        acc = r_ref[...] + acc
    if has_norm:
        _store_with_norm(acc, g_ref, o_ref, hn_ref)
    else:
        o_ref[...] = acc.astype(o_ref.dtype)


def gated_merge(h, wg, wb, branches, *, out_dtype, res=None, norm_gain=None, full_rows=False, row0=0, nrows=None):
    h_from_res = h.ndim == 1
    if h_from_res:
        h = h.reshape(1, -1).astype(F32)
        full_rows = True
    d = h.shape[1]
    m = nrows if nrows is not None else (res.shape[0] if h_from_res else h.shape[0])
    nb, _, n = wg.shape
    kb = wb.shape[1]
    has_norm = norm_gain is not None
    full_rows = full_rows or has_norm
    tm = _pick(math.gcd(m, row0), ((GATED_ROW_TM if full_rows else MM_TM), 256, 128, 64, 8))
    tn = n if full_rows else _pick(n, (GATED_TN, 128))
    i0 = row0 // tm
    npart = len(branches[0])
    rows0 = branches[0][0].shape[0]
    split = rows0 // tm if npart == 2 else m // tm
    assert npart in (1, 2) and rows0 % tm == 0
    in_specs = [pl.BlockSpec((1, d), lambda i, j: (0, 0)) if h_from_res
                else pl.BlockSpec((tm, d), lambda i, j: (i0 + i, 0)),
                pl.BlockSpec((nb, d, tn), lambda i, j: (0, 0, j)),
                pl.BlockSpec((nb, kb, tn), lambda i, j: (0, 0, j))]
    args = [h, wg, wb]
    for parts in branches:
        in_specs.append(pl.BlockSpec((tm, kb), lambda i, j: (jnp.minimum(i, split - 1), 0)))
        if npart == 2:
            in_specs.append(pl.BlockSpec((tm, kb), lambda i, j: (jnp.maximum(i - split, 0), 0)))
        args.extend(parts)
    if res is not None:
        in_specs.append(pl.BlockSpec((tm, tn), lambda i, j: (i0 + i, j)))
        args.append(res)
    out_shape = jax.ShapeDtypeStruct((m, n), out_dtype)
    out_specs = pl.BlockSpec((tm, tn), lambda i, j: (i, j))
    if has_norm:
        in_specs.append(pl.BlockSpec((1, n), lambda i, j: (0, 0)))
        args.append(norm_gain.reshape(1, n).astype(F32))
        out_shape = (out_shape, jax.ShapeDtypeStruct((m, n), BF16))
        out_specs = (out_specs, pl.BlockSpec((tm, tn), lambda i, j: (i, j)))
    return pl.pallas_call(
        functools.partial(_gated_body, nb=nb, npart=npart, split=split, has_res=res is not None,
                          has_norm=has_norm, h_from_res=h_from_res),
        out_shape=out_shape,
        grid=(m // tm, n // tn),
        in_specs=in_specs,
        out_specs=out_specs,
        compiler_params=_cparams(("parallel", "parallel")),
        name="gated_merge",
    )(*args)


PAIR = 2 * LANES


def _rope_rot(y, cos, sin_signed):
    first = (lax.broadcasted_iota(jnp.int32, (1, LANES), 1) % 64) < 32
    parts = []
    for c0 in range(0, y.shape[1], LANES):
        yb = y[:, c0:c0 + LANES]
        parts.append(jnp.where(first, pltpu.roll(yb, 96, 1), pltpu.roll(yb, 32, 1)))
    return y * cos + jnp.concatenate(parts, axis=1) * sin_signed


def _group_mean_matrix(width):
    shift = width.bit_length() - 1
    r = lax.broadcasted_iota(jnp.int32, (PAIR, PAIR), 0) >> shift
    c = lax.broadcasted_iota(jnp.int32, (PAIR, PAIR), 1) >> shift
    return jnp.where(r == c, 1.0 / width, 0.0).astype(BF16)


def _group_meansq(x, mat):
    return jnp.dot((x * x).astype(BF16), mat, preferred_element_type=F32)


def _inv_rms_of_mean(ms, ratio=1.0):
    return lax.rsqrt((ms if ratio == 1.0 else ms * ratio) + NORM_EPS)


def _prep_body(proj_ref, kr_ref, cosb_ref, sinb_ref, cosc_ref, sinc_ref, g256_ref, g512_ref, wuq_ref, wukv_ref,
               qa_ref, ka_ref, va_ref, qb_ref, kb_ref, vb_ref, qc_ref, kc_ref, vc_ref,
               qd0_ref, kd0_ref, vd0_ref, qd1_ref, kd1_ref, vd1_ref, qd2_ref, kd2_ref, vd2_ref,
               sq1, sq2, sk, sv):
    tm = proj_ref.shape[0]
    mean64, mean128, mean256 = _group_mean_matrix(DK_A), _group_mean_matrix(LANES), _group_mean_matrix(PAIR)
    blk = lambda c0, j: slice(c0 + j * LANES, c0 + (j + 1) * LANES)
    pair = lambda c0, j: slice(c0 + j * PAIR, c0 + (j + 1) * PAIR)
    gain = lambda r: g256_ref[r:r + 1, :]
    twice = lambda t: jnp.concatenate([t, t], axis=1)

    def normed(c0, j, mat, grow):
        x = proj_ref[:, pair(c0, j)]
        return x * _inv_rms_of_mean(_group_meansq(x, mat)) * gain(grow)

    for base, grow, dst in ((COL_A, 0, qa_ref), (COL_A + 512, 1, ka_ref)):
        for j in range(H_A // 2):
            dst[:, pair(0, j)] = normed(base, j, mean64, grow).astype(BF16)
    va_ref[...] = proj_ref[:, COL_A + 1024:COL_A + 1536].astype(BF16)

    cosc, sinc = twice(cosc_ref[...]), twice(sinc_ref[...])
    for base, nh, grow, dst in ((COL_C, H_C, 2, qc_ref), (COL_C + 512, KV_C, 3, kc_ref)):
        for j in range(nh // 2):
            dst[:, pair(0, j)] = _rope_rot(normed(base, j, mean128, grow), cosc, sinc).astype(BF16)
    vc_ref[...] = proj_ref[:, COL_C + 768:COL_C + 1024].astype(BF16)

    for j in range(H_D // 2):
        qd0_ref[0, :, pair(0, j)] = normed(COL_D, j, mean128, 4).astype(BF16)
        yk = normed(COL_D + 1536, j, mean128, 5)
        kd0_ref[0, :, pair(0, j)] = yk.astype(BF16)
        for scr, y in ((sq1, normed(COL_D + 512, j, mean128, 4)),
                       (sq2, normed(COL_D + 1024, j, mean128, 4)), (sk, yk)):
            scr[2 * j] = y[:, :LANES]
            scr[2 * j + 1] = y[:, LANES:]
    for h in range(H_D):
        sv[h] = proj_ref[:, blk(COL_D + 2048, h)]
    vd0_ref[0] = proj_ref[:, COL_D + 2048:COL_D + 2560].astype(BF16)
    for (_, dil), dsts in zip(DILATIONS[1:], (((sq1, qd1_ref), (sk, kd1_ref), (sv, vd1_ref)),
                                              ((sq2, qd2_ref), (sk, kd2_ref), (sv, vd2_ref)))):
        for src, dst in dsts:
            for h in range(H_D):
                for r in range(dil):
                    dst[r, :, blk(0, h)] = src[h, pl.ds(r, tm // dil, stride=dil), :].astype(BF16)

    def latent(c0, grow):
        x = proj_ref[:, c0:c0 + 2 * PAIR]
        ms = _group_meansq(x[:, :PAIR], mean256) + _group_meansq(x[:, PAIR:], mean256)
        return (x * twice(_inv_rms_of_mean(ms, 0.5)) * g512_ref[grow:grow + 1, :]).astype(BF16)

    qraw = jnp.dot(latent(COL_BCQ, 0), wuq_ref[...], preferred_element_type=F32)
    kvraw = jnp.dot(latent(COL_BCKV, 1), wukv_ref[...], preferred_element_type=F32)
    kr = kr_ref[...]
    cosb, sinb = cosb_ref[...], sinb_ref[...]
    gq, gk = g512_ref[2:3, 0:PAIR], g512_ref[2:3, PAIR:2 * PAIR]
    for h in range(H_B):
        xq = qraw[:, pair(0, h)]
        xk = jnp.concatenate([kvraw[:, blk(0, 2 * h)], kr], axis=1)
        for x, g, dst in ((xq, gq, qb_ref), (xk, gk, kb_ref)):
            y = x * _inv_rms_of_mean(_group_meansq(x, mean256), PAIR / (DN_B + DR_B)) * g
            dst[:, blk(0, 2 * h)] = y[:, :LANES].astype(BF16)
            dst[:, blk(0, 2 * h + 1)] = _rope_rot(y[:, LANES:], cosb, sinb).astype(BF16)
        vb_ref[:, blk(0, h)] = kvraw[:, blk(0, 2 * h + 1)].astype(BF16)


def _rope_lanes(pos):
    half = DR_B // 2
    inv = jnp.power(ROPE_THETA, -2.0 * jnp.arange(half, dtype=F32) / DR_B)
    ang = pos[:, None] * inv[None, :]
    cos, sin = jnp.cos(ang), jnp.sin(ang)
    return jnp.concatenate([cos, cos], axis=1), jnp.concatenate([-sin, sin], axis=1)


def prepare_mixer_inputs(proj, kr, groups, gains256, gains512, w_uq_p, w_ukv):
    m = proj.shape[0]
    tm = _pick(math.gcd(*[s for _, s in groups]), (PREP_TM, 128, 64, 8))
    smax = max(s for _, s in groups)
    pos = jnp.arange(smax, dtype=jnp.int32)
    cb, sb = _rope_lanes(pos.astype(F32))
    cosb = jnp.concatenate([cb, jnp.ones((smax, 64), F32)], axis=1)
    sinb = jnp.concatenate([sb, jnp.zeros((smax, 64), F32)], axis=1)
    cr, sr = _rope_lanes((pos // GRID_W).astype(F32))
    cc, sc = _rope_lanes((pos % GRID_W).astype(F32))
    cosc = jnp.concatenate([cr, cc], axis=1)
    sinc = jnp.concatenate([sr, sc], axis=1)
    (b0, s0), (_, s1) = groups
    t0, q1 = b0 * s0 // tm, s1 // tm

    def pos_block(i):
        return (jnp.where(i < t0, i % (s0 // tm), (i - t0) % q1), 0)

    row = lambda c: pl.BlockSpec((tm, c), lambda i: (i, 0))
    full = lambda a: pl.BlockSpec(a.shape, lambda i: (0,) * a.ndim)
    tab = pl.BlockSpec((tm, LANES), pos_block)
    wd = H_D * DH_D
    widths = (512, 512, 512, H_B * DKP_B, H_B * DKP_B, H_B * DV_B, 512, 256, 256)
    out_shape = [jax.ShapeDtypeStruct((m, c), BF16) for c in widths]
    out_specs = [row(c) for c in widths]
    for _, dil in DILATIONS:
        out_shape += [jax.ShapeDtypeStruct((dil, m // dil, wd), BF16)] * 3
        out_specs += [pl.BlockSpec((dil, tm // dil, wd), lambda i: (0, i, 0))] * 3
    return pl.pallas_call(
        _prep_body,
        out_shape=tuple(out_shape),
        grid=(m // tm,),
        in_specs=[row(N_IN), row(LANES), tab, tab, tab, tab, full(gains256), full(gains512),
                  full(w_uq_p), full(w_ukv)],
        out_specs=tuple(out_specs),
        scratch_shapes=[pltpu.VMEM((H_D, tm, DH_D), F32)] * 4,
        compiler_params=_cparams(("parallel",)),
        name="prepare_mixer_inputs",
    )(proj, kr, cosb, sinb, cosc, sinc, gains256, gains512, w_uq_p, w_ukv)


def _softmax_step_t(k, q, v, m_ref, l_ref, acc_ref, bias=None):
    st = _kq(k, q)
    if bias is not None:
        st = st + bias
    m_prev = m_ref[...]
    m_new = jnp.maximum(m_prev, jnp.max(st, axis=0, keepdims=True))
    alpha = jnp.exp2(m_prev - m_new)
    pt = jnp.exp2(st - m_new)
    l_ref[...] = alpha * l_ref[...] + jnp.sum(pt, axis=0, keepdims=True)
    pv = lax.dot_general(v, pt.astype(BF16), (((0,), (0,)), ((), ())), preferred_element_type=F32)
    acc_ref[...] = alpha * acc_ref[...] + pv
    m_ref[...] = m_new


def _init_stats(m_ref, l_ref, acc_ref):
    m_ref[...] = jnp.full(m_ref.shape, -jnp.inf, F32)
    l_ref[...] = jnp.zeros(l_ref.shape, F32)
    acc_ref[...] = jnp.zeros(acc_ref.shape, F32)


def _kq(k, q):
    return lax.dot_general(k, q, (((1,), (1,)), ((), ())), preferred_element_type=F32)


def _attn_body(q_ref, k_ref, v_ref, o_ref, m_ref, l_ref, acc_ref, *, tk, nk):
    _init_stats(m_ref, l_ref, acc_ref)
    q = q_ref[...]

    def step(j, carry):
        off = pl.multiple_of(j * tk, tk)
        _softmax_step_t(k_ref[pl.ds(off, tk), :], q, v_ref[pl.ds(off, tk), :], m_ref, l_ref, acc_ref)
        return carry

    lax.fori_loop(0, nk, step, 0)
    o_ref[...] = jnp.transpose(acc_ref[...] / l_ref[...]).astype(o_ref.dtype)


def _attn_scratch(tq, dv):
    return [pltpu.VMEM((1, tq), F32), pltpu.VMEM((1, tq), F32), pltpu.VMEM((dv, tq), F32)]


def _attn_tiles(seq):
    return _pick(seq, (ATTN_TQ, 1024, 512, 256, 128)), _pick(seq, (ATTN_TK, 256, 128))


def dense_attention(q, k, v, *, row0, nseq, seq, hq, hk, dk, dv):
    tq, tk = _attn_tiles(seq)
    nq = seq // tq
    group = hq // hk
    qb0 = row0 // tq
    sb0 = row0 // seq
    return pl.pallas_call(
        functools.partial(_attn_body, tk=tk, nk=seq // tk),
        out_shape=jax.ShapeDtypeStruct((nseq * seq, hq * dv), BF16),
        grid=(nseq, hq, nq),
        in_specs=[pl.BlockSpec((tq, dk), lambda b, h, i: (qb0 + b * nq + i, h)),
                  pl.BlockSpec((seq, dk), lambda b, h, i: (sb0 + b, h // group)),
                  pl.BlockSpec((seq, dv), lambda b, h, i: (sb0 + b, h // group))],
        out_specs=pl.BlockSpec((tq, dv), lambda b, h, i: (b * nq + i, h)),
        scratch_shapes=_attn_scratch(tq, dv),
        compiler_params=_cparams(("parallel", "parallel", "parallel")),
        name="dense_attention",
    )(q, k, v)


BIAS_FAR = 1152


def _diff_body(lam_ref, q_ref, k_ref, v_ref, bias_ref, g_ref, o_ref,
               m1, l1, a1, m2, l2, a2, *, tq, tk, nk, lo, out_scale):
    _init_stats(m1, l1, a1)
    _init_stats(m2, l2, a2)
    q = q_ref[...]
    lane = lax.broadcasted_iota(jnp.int32, q.shape, 1)
    zero = jnp.zeros_like(q)
    q1 = jnp.where(lane < DK_A, q, zero)
    q2 = jnp.where(lane >= DK_A, q, zero)
    q0 = pl.program_id(2) * tq

    def step(j, carry):
        off = pl.multiple_of(j * tk, tk)
        k = k_ref[pl.ds(off, tk), :]
        v = v_ref[pl.ds(off, tk), :]
        cols = []
        for c0 in range(0, tq, LANES):
            r0 = jnp.clip(off - q0 - c0, lo, BIAS_FAR) - lo
            cols.append(bias_ref[pl.ds(pl.multiple_of(r0, LANES), tk), :])
        bias = jnp.concatenate(cols, axis=1)
        _softmax_step_t(k, q1, v, m1, l1, a1, bias)
        _softmax_step_t(k, q2, v, m2, l2, a2, bias)
        return carry

    lax.fori_loop(0, nk, step, 0)
    o = a1[...] / l1[...] - lam_ref[0] * (a2[...] / l2[...])
    y = o * lax.rsqrt(jnp.mean(o * o, axis=0, keepdims=True) + NORM_EPS)
    o_ref[...] = jnp.transpose((y * g_ref[...]) * out_scale).astype(o_ref.dtype)


def _rel_bucket(rel):
    half = N_BUCKETS // 2
    exact = half // 2
    n = jnp.abs(rel)
    nf = jnp.maximum(n, 1).astype(F32)
    large = exact + (jnp.log(nf / exact) / math.log(REL_MAX_DIST / exact) * (half - exact)).astype(jnp.int32)
    large = jnp.minimum(large, half - 1)
    return jnp.where(rel > 0, half, 0) + jnp.where(n < exact, n, large)


def _skew(vec, nrows):
    p = vec.shape[-1]
    flat = jnp.tile(vec, (1,) * (vec.ndim - 1) + (nrows,))[..., :nrows * (p - 1)]
    return flat.reshape(vec.shape[:-1] + (nrows, p - 1))


def _diff_bias_geometry(tk):
    lo = -(REL_MAX_DIST + tk)
    return lo, BIAS_FAR - lo + tk


def _diagonal_table(tbl, tk):
    lo, nrows = _diff_bias_geometry(tk)
    rel = jnp.arange(lo - (LANES - 1), lo + nrows + 1, dtype=jnp.int32)
    vec = jnp.transpose(tbl[_rel_bucket(rel)]).astype(F32) * LOG2E
    diag = _skew(vec, LANES)[:, :, LANES - 1:LANES - 1 + nrows]
    return jnp.transpose(diag, (0, 2, 1))


def diff_attention(q, k, v, table, lam, out_g, *, row0, nseq, seq, lam_init):
    tq, tk = _attn_tiles(seq)
    nq = seq // tq
    qb0 = row0 // tq
    sb0 = row0 // seq
    lo, nrows = _diff_bias_geometry(tk)
    assert table.shape == (H_A, nrows, LANES)
    return pl.pallas_call(
        functools.partial(_diff_body, tq=tq, tk=tk, nk=seq // tk, lo=lo, out_scale=1.0 - lam_init),
        out_shape=jax.ShapeDtypeStruct((nseq * seq, H_A * DV_A), BF16),
        grid=(nseq, H_A, nq),
        in_specs=[pl.BlockSpec(memory_space=pltpu.SMEM),
                  pl.BlockSpec((tq, 2 * DK_A), lambda b, h, i: (qb0 + b * nq + i, h)),
                  pl.BlockSpec((seq, 2 * DK_A), lambda b, h, i: (sb0 + b, h)),
                  pl.BlockSpec((seq, DV_A), lambda b, h, i: (sb0 + b, h)),
                  pl.BlockSpec((None, nrows, LANES), lambda b, h, i: (h, 0, 0)),
                  pl.BlockSpec((DV_A, 1), lambda b, h, i: (0, 0))],
        out_specs=pl.BlockSpec((tq, DV_A), lambda b, h, i: (b * nq + i, h)),
        scratch_shapes=_attn_scratch(tq, DV_A) + _attn_scratch(tq, DV_A),
        compiler_params=_cparams(("parallel", "parallel", "parallel")),
        name="diff_attention",
    )(lam, q, k, v, table, out_g.reshape(DV_A, 1).astype(F32))


def _band_body(q_ref, kp_ref, kc_ref, kn_ref, vp_ref, vc_ref, vn_ref, bias_ref, o_ref, lse_ref,
               *, tu, nb):
    n = pl.program_id(2)
    tq = min(tu, 2 * HALO)
    tkk = tq + 2 * HALO
    col = lax.broadcasted_iota(jnp.int32, (1, tkk), 1)
    for h in range(H_D):
        hs = slice(h * DH_D, (h + 1) * DH_D)
        k = jnp.concatenate([kp_ref[:, hs], kc_ref[:, hs], kn_ref[:, hs]], axis=0)
        v = jnp.concatenate([vp_ref[:, hs], vc_ref[:, hs], vn_ref[:, hs]], axis=0)
        for q0 in range(0, tu, tq):
            s = _kq(q_ref[q0:q0 + tq, hs], k[q0:q0 + tkk]) + bias_ref[h, 0:tq, 0:tkk]
            if q0 == 0:
                s = jnp.where((col < HALO) & (n == 0), NEG_INF, s)
            if q0 + tq == tu:
                s = jnp.where((col >= tq + HALO) & (n == nb - 1), NEG_INF, s)
            m = jnp.max(s, axis=1, keepdims=True)
            p = jnp.exp2(s - m)
            l = jnp.sum(p, axis=1, keepdims=True)
            o = jnp.dot(p.astype(BF16), v[q0:q0 + tkk], preferred_element_type=F32) / l
            o_ref[q0:q0 + tq, hs] = o.astype(o_ref.dtype)
            lse_ref[q0:q0 + tq, hs] = jnp.broadcast_to(m + jnp.log2(l), (tq, DH_D))


def _band_table(bias_d, tu):
    p = 2 * tu + 2 * HALO + 1
    x = jnp.arange(p, dtype=jnp.int32)
    vecs = []
    for g, (_, dil) in enumerate(DILATIONS):
        tbl = bias_d[:, g * H_D:(g + 1) * H_D]
        band = jnp.transpose(tbl[_rel_bucket((x - HALO) * dil)]).astype(F32) * LOG2E
        vecs.append(jnp.where((x <= 2 * HALO)[None], band, NEG_INF))
    return _skew(jnp.stack(vecs), tu)[..., :tu + 2 * HALO]


def band_attention(q, k, v, table, g, *, row0, nseq, seq):
    dil, rows, width = q.shape
    length = seq // dil
    tu = table.shape[2]
    nb = length // tu
    u0 = row0 // dil
    hpt = tu // HALO
    last_halo = rows // HALO - 1

    def tile(b, n):
        return u0 // tu + b * nb + n

    cur = pl.BlockSpec((None, tu, width), lambda b, r, n: (r, tile(b, n), 0))
    prev = pl.BlockSpec((None, HALO, width), lambda b, r, n: (r, jnp.maximum(tile(b, n) * hpt - 1, 0), 0))
    nxt = pl.BlockSpec((None, HALO, width),
                       lambda b, r, n: (r, jnp.minimum((tile(b, n) + 1) * hpt, last_halo), 0))
    out_rows = nseq * length
    out_spec = pl.BlockSpec((None, tu, width), lambda b, r, n: (r, b * nb + n, 0))
    return pl.pallas_call(
        functools.partial(_band_body, tu=tu, nb=nb),
        out_shape=(jax.ShapeDtypeStruct((dil, out_rows, width), F32),
                   jax.ShapeDtypeStruct((dil, out_rows, width), F32)),
        grid=(nseq, dil, nb),
        in_specs=[cur, prev, cur, nxt, prev, cur, nxt,
                  pl.BlockSpec((None, H_D, tu, tu + 2 * HALO), lambda b, r, n: (g, 0, 0, 0))],
        out_specs=(out_spec, out_spec),
        compiler_params=_cparams(("parallel", "parallel", "parallel")),
        name="band_attention",
    )(q, k, k, k, v, v, v, table)


def _mix_body(*refs):
    o_refs = refs[:N_DIL]
    l_refs = refs[N_DIL:2 * N_DIL]
    out_ref = refs[2 * N_DIL]
    scratch = refs[2 * N_DIL + 1:]
    tm = out_ref.shape[0]
    vals = [o_refs[0][0], l_refs[0][0]]
    for g, (_, dil) in enumerate(DILATIONS[1:], start=1):
        for src, dst in ((o_refs[g], scratch[2 * g - 2]), (l_refs[g], scratch[2 * g - 1])):
            for h in range(H_D):
                for r in range(dil):
                    dst[h, pl.ds(r, tm // dil, stride=dil), :] = src[r, :, h * DH_D:(h + 1) * DH_D]
            vals.append(jnp.concatenate([dst[h] for h in range(H_D)], axis=1))
    outs, ls = vals[0::2], vals[1::2]
    mx = functools.reduce(jnp.maximum, ls)
    es = [jnp.exp2(l - mx) for l in ls]
    den = functools.reduce(lambda a, b: a + b, es)
    acc = None
    for e, o in zip(es, outs):
        term = (e / den) * o
        acc = term if acc is None else acc + term
    out_ref[...] = acc.astype(out_ref.dtype)


def band_mixture(outs, lses):
    _, m, c = outs[0].shape
    tm = _pick(m, (PREP_TM, 128, 64))
    specs = [pl.BlockSpec((dil, tm // dil, c), lambda i: (0, i, 0)) for _, dil in DILATIONS]
    return pl.pallas_call(
        _mix_body,
        out_shape=jax.ShapeDtypeStruct((m, c), BF16),
        grid=(m // tm,),
        in_specs=specs * 2,
        out_specs=pl.BlockSpec((tm, c), lambda i: (i, 0)),
        scratch_shapes=[pltpu.VMEM((H_D, tm, DH_D), F32)] * (2 * (N_DIL - 1)),
        compiler_params=_cparams(("parallel",)),
        name="band_mixture",
    )(*outs, *lses)


def kernel(x_prompt, x_sample, p_prompt, p_sample, rel_bias, norm_mix, w_in, a_q_norm, a_k_norm, a_lambda_q1, a_lambda_k1, a_lambda_q2, a_lambda_k2, a_out_norm, b_cq_norm, b_ckv_norm, b_w_uq, b_w_ukv, b_q_norm, b_k_norm, c_q_norm, c_k_norm, d_q_norm, d_k_norm, w_gate, w_branch, w_out, norm_ffn, w_ff1, w_ff2, norm_ple, w_ple_gate, w_ple_proj):
    depth = w_in.shape[0]
    d_model = x_prompt.shape[-1]
    groups = [(x_prompt.shape[0], x_prompt.shape[1]), (x_sample.shape[0], x_sample.shape[1])]
    row0s = [0, groups[0][0] * groups[0][1]]
    x = jnp.concatenate([x_prompt.reshape(-1, d_model), x_sample.reshape(-1, d_model)], axis=0)
    p_parts = [p_prompt.reshape(depth, -1, p_prompt.shape[-1]), p_sample.reshape(depth, -1, p_sample.shape[-1])]

    in_sizes = (512, 512, 512, 512, 512, 64, 512, 256, 256, 1536, 512, 512)
    offs = np.concatenate([[0], np.cumsum(in_sizes)])
    seg = lambda w, a, b: w[:, offs[a]:offs[b]]

    table_a = {}
    tables_d = {}
    for _, seq in groups:
        tk = _attn_tiles(seq)[1]
        if tk not in table_a:
            table_a[tk] = _diagonal_table(rel_bias[:, :H_A], tk)
        for _, dil in DILATIONS:
            tu = min(BAND_TU, seq // dil)
            if tu not in tables_d:
                tables_d[tu] = _band_table(rel_bias[:, H_A:], tu)

    two = lambda g: jnp.concatenate([g, g])
    pad_b = lambda g: jnp.pad(g, (0, DKP_B - DN_B - DR_B))
    h = rmsnorm(x, norm_mix[0])
    for i in range(depth):
        lam_init = 0.8 - 0.6 * math.exp(-0.3 * i)
        wi = w_in[i]
        w_in_p = jnp.concatenate([seg(wi, 0, 3), seg(wi, 6, 9), seg(wi, 9, 12), seg(wi, 3, 5)], axis=1).astype(BF16)
        w_kr_p = jnp.pad(seg(wi, 5, 6), ((0, 0), (0, LANES - DR_B))).astype(BF16)
        w_uq_p = jnp.pad(b_w_uq[i].reshape(Q_RANK_B, H_B, DN_B + DR_B),
                         ((0, 0), (0, 0), (0, DKP_B - DN_B - DR_B))).reshape(Q_RANK_B, H_B * DKP_B).astype(BF16)
        zeros = jnp.zeros((PAIR,), F32)
        gains256 = jnp.stack([two(two(a_q_norm[i])) * (DK_A ** -0.5 * LOG2E), two(two(a_k_norm[i])),
                              two(c_q_norm[i]) * (DH_C ** -0.5 * LOG2E), two(c_k_norm[i]),
                              two(d_q_norm[i]) * (DH_D ** -0.5 * LOG2E), two(d_k_norm[i]), zeros, zeros]).astype(F32)
        gains512 = jnp.stack([b_cq_norm[i], b_ckv_norm[i],
                              jnp.concatenate([pad_b(b_q_norm[i]) * ((DN_B + DR_B) ** -0.5 * LOG2E),
                                               pad_b(b_k_norm[i])])]
                             + [jnp.zeros((512,), F32)] * 5).astype(F32)
        lam = (jnp.exp(jnp.sum(a_lambda_q1[i].astype(F32) * a_lambda_k1[i].astype(F32)))
               - jnp.exp(jnp.sum(a_lambda_q2[i].astype(F32) * a_lambda_k2[i].astype(F32))) + lam_init)
        lam = lam.reshape(1).astype(F32)

        proj = matmul(h, w_in_p, out_dtype=F32)
        kr = matmul(h, w_kr_p, out_dtype=F32)
        (qa, ka, va, qb, kb, vb, qc, kc, vc, *qkv_d) = prepare_mixer_inputs(
            proj, kr, groups, gains256, gains512, w_uq_p, b_w_ukv[i].astype(BF16))

        branches = [[], [], [], []]
        for (nseq, seq), row0 in zip(groups, row0s):
            kw = dict(row0=row0, nseq=nseq, seq=seq)
            branches[0].append(diff_attention(qa, ka, va, table_a[_attn_tiles(seq)[1]], lam, a_out_norm[i],
                                              lam_init=lam_init, **kw))
            branches[1].append(dense_attention(qb, kb, vb, hq=H_B, hk=H_B, dk=DKP_B, dv=DV_B, **kw))
            branches[2].append(dense_attention(qc, kc, vc, hq=H_C, hk=KV_C, dk=DH_C, dv=DH_C, **kw))
            outs, lses = zip(*[band_attention(*qkv_d[3 * g:3 * g + 3], tables_d[min(BAND_TU, seq // dil)], g, **kw)
                               for g, (_, dil) in enumerate(DILATIONS)])
            branches[3].append(band_mixture(outs, lses))

        merged = gated_merge(h, w_gate[i].astype(BF16), w_branch[i].astype(BF16), branches, out_dtype=BF16)
        x, h2 = matmul(merged, w_out[i].astype(BF16), out_dtype=F32, res=x, norm_gain=norm_ffn[i])
        u = matmul(h2, w_ff1[i].astype(BF16), out_dtype=BF16, act="relu2")
        x = matmul(u, w_ff2[i].astype(BF16), out_dtype=F32, res=x, tm=512, tn=512, tk=u.shape[1])
        ple = dict(out_dtype=F32, res=x)
        ple_w = (norm_ple[i], w_ple_gate[i].astype(BF16)[None], w_ple_proj[i].astype(BF16)[None])
        if i + 1 < depth:
            x, h = gated_merge(*ple_w, [[p[i] for p in p_parts]], norm_gain=norm_mix[i + 1], **ple)
        else:
            ys = [gated_merge(*ple_w, [[p[i]]], full_rows=True, row0=row0, nrows=nseq * seq, **ple)
                  for p, (nseq, seq), row0 in zip(p_parts, groups, row0s)]

    y_prompt, y_sample = ys[0].reshape(x_prompt.shape), ys[1].reshape(x_sample.shape)
    return (y_prompt, y_sample)
```

```python
import functools
import math

import jax
import jax.numpy as jnp
import numpy as np
from jax import lax
from jax.experimental import pallas as pl
from jax.experimental.pallas import tpu as pltpu

BF16 = jnp.bfloat16
F32 = jnp.float32

GRID_W = 64
H_A, DK_A, DV_A = 4, 64, 128
H_B, DN_B, DR_B, DV_B = 4, 128, 64, 128
Q_RANK_B = KV_RANK_B = 512
H_C, KV_C, DH_C = 4, 2, 128
H_D, DH_D = 4, 128
DILATIONS = ((128, 1), (512, 4), (2048, 16))
N_DIL = len(DILATIONS)
HALO = 64
assert all(w // (2 * d) == HALO for w, d in DILATIONS)
N_BUCKETS = 32
REL_MAX_DIST = 1024
ROPE_THETA = 10000.0
NORM_EPS = 1e-6
NEG_INF = -1e30
LOG2E = 1.4426950408889634
LANES = 128
DKP_B = 256
VMEM_LIMIT = 56 * 1024 * 1024

MM_TM, MM_TN, MM_TK = 1024, 1024, 2048
ROW_TM = 512
GATED_TN = 256
GATED_ROW_TM = 256
ATTN_TQ, ATTN_TK = 2048, 1024
PREP_TM = 256
BAND_TU = 256

COL_A, COL_C, COL_D, COL_BCQ, COL_BCKV, N_IN = 0, 1536, 2560, 5120, 5632, 6144


def _cparams(sem):
    return pltpu.CompilerParams(dimension_semantics=sem, vmem_limit_bytes=VMEM_LIMIT)


def _pick(n, prefs):
    for t in prefs:
        if n % t == 0:
            return t
    return n


def _piece_specs(shape, split, col):
    return [pl.BlockSpec(shape, lambda i, *rest: (jnp.minimum(i, split - 1), col(*rest))),
            pl.BlockSpec(shape, lambda i, *rest: (jnp.maximum(i - split, 0), col(*rest)))]


def _read_pieces(refs, split):
    x = refs[0][...]
    if len(refs) == 2:
        x = jnp.where(pl.program_id(0) < split, x, refs[1][...])
    return x


def _rmsnorm_body(*refs, npart, split):
    g_ref, o_ref = refs[npart:]
    x = _read_pieces(refs[:npart], split)
    y = x * lax.rsqrt(jnp.mean(x * x, axis=-1, keepdims=True) + NORM_EPS)
    o_ref[...] = (y * g_ref[...]).astype(o_ref.dtype)


def rmsnorm(pieces, g, out_dtype=BF16):
    d = pieces[0].shape[1]
    rows = [p.shape[0] for p in pieces]
    m = sum(rows)
    tm = _pick(math.gcd(*rows, m), (512, 256, 128, 64, 8))
    split = rows[0] // tm
    specs = (_piece_specs((tm, d), split, lambda: 0) if len(pieces) == 2
             else [pl.BlockSpec((tm, d), lambda i: (i, 0))])
    return pl.pallas_call(
        functools.partial(_rmsnorm_body, npart=len(pieces), split=split),
        out_shape=jax.ShapeDtypeStruct((m, d), out_dtype),
        grid=(m // tm,),
        in_specs=specs + [pl.BlockSpec((1, d), lambda i: (0, 0))],
        out_specs=pl.BlockSpec((tm, d), lambda i: (i, 0)),
        compiler_params=_cparams(("parallel",)),
        name="rmsnorm",
    )(*pieces, g.reshape(1, d).astype(F32))


def _store_with_norm(acc, g_ref, o_ref, h_ref):
    o_ref[...] = acc
    inv = lax.rsqrt(jnp.mean(acc * acc, axis=-1, keepdims=True) + NORM_EPS)
    h_ref[...] = (acc * inv * g_ref[...]).astype(h_ref.dtype)


def _mm_body(*refs, act, n_res, split, has_norm, nk):
    a_ref, w_ref = refs[:2]
    n_in = 2 + n_res + has_norm
    r_refs = refs[2:2 + n_res]
    g_ref = refs[n_in - 1] if has_norm else None
    o_ref = refs[n_in]
    h_ref = refs[n_in + 1] if has_norm else None
    scratch = refs[n_in + 1 + has_norm:]

    def finish(acc):
        if act == "relu2":
            acc = jnp.square(jnp.maximum(acc, 0.0))
        if n_res:
            acc = _read_pieces(r_refs, split) + acc
        if has_norm:
            _store_with_norm(acc, g_ref, o_ref, h_ref)
        else:
            o_ref[...] = acc.astype(o_ref.dtype)

    part = jnp.dot(a_ref[...], w_ref[...], preferred_element_type=F32)
    if nk == 1:
        finish(part)
        return
    acc_ref, = scratch
    k = pl.program_id(2)

    @pl.when(k == 0)
    def _():
        acc_ref[...] = part

    @pl.when(k > 0)
    def _():
        acc_ref[...] += part

    @pl.when(k == nk - 1)
    def _():
        finish(acc_ref[...])


def matmul(a, w, *, out_dtype, act=None, res=None, norm_gain=None, tm=MM_TM, tn=MM_TN, tk=MM_TK):
    m, kdim = a.shape
    n = w.shape[1]
    has_norm = norm_gain is not None
    tm = _pick(m, ((ROW_TM if has_norm else tm), 512, 256, 128, 64, 8))
    tn = n if has_norm else _pick(n, (tn, 512, 256, 128))
    tk = _pick(kdim, (tk, 2048, 1024, 512))
    nk = kdim // tk
    in_specs = [pl.BlockSpec((tm, tk), lambda i, j, k: (i, k)),
                pl.BlockSpec((tk, tn), lambda i, j, k: (k, j))]
    args = [a, w]
    res = [] if res is None else (list(res) if isinstance(res, (list, tuple)) else [res])
    split = res[0].shape[0] // tm if len(res) == 2 else 0
    if len(res) == 2:
        assert res[0].shape[0] % tm == 0
        in_specs += _piece_specs((tm, tn), split, lambda j, k: j)
    elif res:
        in_specs.append(pl.BlockSpec((tm, tn), lambda i, j, k: (i, j)))
    args += res
    out_shape = jax.ShapeDtypeStruct((m, n), out_dtype)
    out_specs = pl.BlockSpec((tm, tn), lambda i, j, k: (i, j))
    if has_norm:
        in_specs.append(pl.BlockSpec((1, n), lambda i, j, k: (0, 0)))
        args.append(norm_gain.reshape(1, n).astype(F32))
        out_shape = (out_shape, jax.ShapeDtypeStruct((m, n), BF16))
        out_specs = (out_specs, pl.BlockSpec((tm, tn), lambda i, j, k: (i, j)))
    return pl.pallas_call(
        functools.partial(_mm_body, act=act, n_res=len(res), split=split, has_norm=has_norm, nk=nk),
        out_shape=out_shape,
        grid=(m // tm, n // tn, nk),
        in_specs=in_specs,
        out_specs=out_specs,
        scratch_shapes=[pltpu.VMEM((tm, tn), F32)] if nk > 1 else [],
        compiler_params=_cparams(("parallel", "parallel", "arbitrary")),
        name="matmul",
    )(*args)


def _gated_body(*refs, nb, npart, split, has_res, has_norm, h_from_res):
    h_ref, wg_ref, wb_ref = refs[:3]
    br_refs = refs[3:3 + nb * npart]
    rest = list(refs[3 + nb * npart:])
    r_ref = rest.pop(0) if has_res else None
    g_ref = rest.pop(0) if has_norm else None
    o_ref = rest.pop(0)
    hn_ref = rest.pop(0) if has_norm else None
    if h_from_res:
        r = r_ref[...]
        h = (r * lax.rsqrt(jnp.mean(r * r, axis=-1, keepdims=True) + NORM_EPS) * h_ref[...]).astype(BF16)
    else:
        h = h_ref[...]
    first = pl.program_id(0) < split
    acc = None
    for b in range(nb):
        parts = br_refs[b * npart:(b + 1) * npart]
        x = parts[0][...]
        if npart == 2:
            x = jnp.where(first, x, parts[1][...])
        gate = jnp.dot(h, wg_ref[b], preferred_element_type=F32)
        val = jnp.dot(x.astype(BF16), wb_ref[b], preferred_element_type=F32)
        term = jax.nn.sigmoid(gate) * val
        acc = term if acc is None else acc + term
    if has_res:
        acc = r_ref[...] + acc
    if has_norm:
        _store_with_norm(acc, g_ref, o_ref, hn_ref)
    else:
        o_ref[...] = acc.astype(o_ref.dtype)


def gated_merge(h, wg, wb, branches, *, out_dtype, res=None, norm_gain=None, full_rows=False, row0=0, nrows=None):
    h_from_res = h.ndim == 1
    if h_from_res:
        h = h.reshape(1, -1).astype(F32)
        full_rows = True
    d = h.shape[1]
    m = nrows if nrows is not None else (res.shape[0] if h_from_res else h.shape[0])
    nb, _, n = wg.shape
    kb = wb.shape[1]
    has_norm = norm_gain is not None
    full_rows = full_rows or has_norm
    tm = _pick(math.gcd(m, row0), ((GATED_ROW_TM if full_rows else MM_TM), 256, 128, 64, 8))
    tn = n if full_rows else _pick(n, (GATED_TN, 128))
    i0 = row0 // tm
    npart = len(branches[0])
    rows0 = branches[0][0].shape[0]
    split = rows0 // tm if npart == 2 else m // tm
    assert npart in (1, 2) and rows0 % tm == 0
    in_specs = [pl.BlockSpec((1, d), lambda i, j: (0, 0)) if h_from_res
                else pl.BlockSpec((tm, d), lambda i, j: (i0 + i, 0)),
                pl.BlockSpec((nb, d, tn), lambda i, j: (0, 0, j)),
                pl.BlockSpec((nb, kb, tn), lambda i, j: (0, 0, j))]
    args = [h, wg, wb]
    for parts in branches:
        in_specs.append(pl.BlockSpec((tm, kb), lambda i, j: (jnp.minimum(i, split - 1), 0)))
        if npart == 2:
            in_specs.append(pl.BlockSpec((tm, kb), lambda i, j: (jnp.maximum(i - split, 0), 0)))
        args.extend(parts)
    if res is not None:
        in_specs.append(pl.BlockSpec((tm, tn), lambda i, j: (i0 + i, j)))
        args.append(res)
    out_shape = jax.ShapeDtypeStruct((m, n), out_dtype)
    out_specs = pl.BlockSpec((tm, tn), lambda i, j: (i, j))
    if has_norm:
        in_specs.append(pl.BlockSpec((1, n), lambda i, j: (0, 0)))
        args.append(norm_gain.reshape(1, n).astype(F32))
        out_shape = (out_shape, jax.ShapeDtypeStruct((m, n), BF16))
        out_specs = (out_specs, pl.BlockSpec((tm, tn), lambda i, j: (i, j)))
    return pl.pallas_call(
        functools.partial(_gated_body, nb=nb, npart=npart, split=split, has_res=res is not None,
                          has_norm=has_norm, h_from_res=h_from_res),
        out_shape=out_shape,
        grid=(m // tm, n // tn),
        in_specs=in_specs,
        out_specs=out_specs,
        compiler_params=_cparams(("parallel", "parallel")),
        name="gated_merge",
    )(*args)


PAIR = 2 * LANES


def _rope_rot(y, cos, sin_signed):
    first = (lax.broadcasted_iota(jnp.int32, (1, LANES), 1) % 64) < 32
    parts = []
    for c0 in range(0, y.shape[1], LANES):
        yb = y[:, c0:c0 + LANES]
        parts.append(jnp.where(first, pltpu.roll(yb, 96, 1), pltpu.roll(yb, 32, 1)))
    return y * cos + jnp.concatenate(parts, axis=1) * sin_signed


def _group_mean_matrix(width):
    shift = width.bit_length() - 1
    r = lax.broadcasted_iota(jnp.int32, (PAIR, PAIR), 0) >> shift
    c = lax.broadcasted_iota(jnp.int32, (PAIR, PAIR), 1) >> shift
    return jnp.where(r == c, 1.0 / width, 0.0).astype(BF16)


def _group_meansq(x, mat):
    return jnp.dot((x * x).astype(BF16), mat, preferred_element_type=F32)


def _inv_rms_of_mean(ms, ratio=1.0):
    return lax.rsqrt((ms if ratio == 1.0 else ms * ratio) + NORM_EPS)


def _prep_body(proj_ref, kr_ref, cosb_ref, sinb_ref, cosc_ref, sinc_ref, g256_ref, g512_ref, wuq_ref, wukv_ref,
               qa_ref, ka_ref, va_ref, qb_ref, kb_ref, vb_ref, qc_ref, kc_ref, vc_ref,
               qd0_ref, kd0_ref, vd0_ref, qd1_ref, kd1_ref, vd1_ref, qd2_ref, kd2_ref, vd2_ref,
               sq1, sq2, sk, sv):
    tm = proj_ref.shape[0]
    mean64, mean128, mean256 = _group_mean_matrix(DK_A), _group_mean_matrix(LANES), _group_mean_matrix(PAIR)
    blk = lambda c0, j: slice(c0 + j * LANES, c0 + (j + 1) * LANES)
    pair = lambda c0, j: slice(c0 + j * PAIR, c0 + (j + 1) * PAIR)
    gain = lambda r: g256_ref[r:r + 1, :]
    twice = lambda t: jnp.concatenate([t, t], axis=1)

    def normed(c0, j, mat, grow):
        x = proj_ref[:, pair(c0, j)]
        return x * _inv_rms_of_mean(_group_meansq(x, mat)) * gain(grow)

    for base, grow, dst in ((COL_A, 0, qa_ref), (COL_A + 512, 1, ka_ref)):
        for j in range(H_A // 2):
            dst[:, pair(0, j)] = normed(base, j, mean64, grow).astype(BF16)
    va_ref[...] = proj_ref[:, COL_A + 1024:COL_A + 1536].astype(BF16)

    cosc, sinc = twice(cosc_ref[...]), twice(sinc_ref[...])
    for base, nh, grow, dst in ((COL_C, H_C, 2, qc_ref), (COL_C + 512, KV_C, 3, kc_ref)):
        for j in range(nh // 2):
            dst[:, pair(0, j)] = _rope_rot(normed(base, j, mean128, grow), cosc, sinc).astype(BF16)
    vc_ref[...] = proj_ref[:, COL_C + 768:COL_C + 1024].astype(BF16)

    for j in range(H_D // 2):
        qd0_ref[0, :, pair(0, j)] = normed(COL_D, j, mean128, 4).astype(BF16)
        yk = normed(COL_D + 1536, j, mean128, 5)
        kd0_ref[0, :, pair(0, j)] = yk.astype(BF16)
        for scr, y in ((sq1, normed(COL_D + 512, j, mean128, 4)),
                       (sq2, normed(COL_D + 1024, j, mean128, 4)), (sk, yk)):
            scr[2 * j] = y[:, :LANES]
            scr[2 * j + 1] = y[:, LANES:]
    for h in range(H_D):
        sv[h] = proj_ref[:, blk(COL_D + 2048, h)]
    vd0_ref[0] = proj_ref[:, COL_D + 2048:COL_D + 2560].astype(BF16)
    for (_, dil), dsts in zip(DILATIONS[1:], (((sq1, qd1_ref), (sk, kd1_ref), (sv, vd1_ref)),
                                              ((sq2, qd2_ref), (sk, kd2_ref), (sv, vd2_ref)))):
        for src, dst in dsts:
            for h in range(H_D):
                for r in range(dil):
                    dst[r, :, blk(0, h)] = src[h, pl.ds(r, tm // dil, stride=dil), :].astype(BF16)

    def latent(c0, grow):
        x = proj_ref[:, c0:c0 + 2 * PAIR]
        ms = _group_meansq(x[:, :PAIR], mean256) + _group_meansq(x[:, PAIR:], mean256)
        return (x * twice(_inv_rms_of_mean(ms, 0.5)) * g512_ref[grow:grow + 1, :]).astype(BF16)

    qraw = jnp.dot(latent(COL_BCQ, 0), wuq_ref[...], preferred_element_type=F32)
    kvraw = jnp.dot(latent(COL_BCKV, 1), wukv_ref[...], preferred_element_type=F32)
    kr = kr_ref[...]
    cosb, sinb = cosb_ref[...], sinb_ref[...]
    gq, gk = g512_ref[2:3, 0:PAIR], g512_ref[2:3, PAIR:2 * PAIR]
    for h in range(H_B):
        xq = qraw[:, pair(0, h)]
        xk = jnp.concatenate([kvraw[:, blk(0, 2 * h)], kr], axis=1)
        for x, g, dst in ((xq, gq, qb_ref), (xk, gk, kb_ref)):
            y = x * _inv_rms_of_mean(_group_meansq(x, mean256), PAIR / (DN_B + DR_B)) * g
            dst[:, blk(0, 2 * h)] = y[:, :LANES].astype(BF16)
            dst[:, blk(0, 2 * h + 1)] = _rope_rot(y[:, LANES:], cosb, sinb).astype(BF16)
        vb_ref[:, blk(0, h)] = kvraw[:, blk(0, 2 * h + 1)].astype(BF16)


def _rope_lanes(pos):
    half = DR_B // 2
    inv = jnp.power(ROPE_THETA, -2.0 * jnp.arange(half, dtype=F32) / DR_B)
    ang = pos[:, None] * inv[None, :]
    cos, sin = jnp.cos(ang), jnp.sin(ang)
    return jnp.concatenate([cos, cos], axis=1), jnp.concatenate([-sin, sin], axis=1)


def prepare_mixer_inputs(proj, kr, groups, gains256, gains512, w_uq_p, w_ukv):
    m = proj.shape[0]
    tm = _pick(math.gcd(*[s for _, s in groups]), (PREP_TM, 128, 64, 8))
    smax = max(s for _, s in groups)
    pos = jnp.arange(smax, dtype=jnp.int32)
    cb, sb = _rope_lanes(pos.astype(F32))
    cosb = jnp.concatenate([cb, jnp.ones((smax, 64), F32)], axis=1)
    sinb = jnp.concatenate([sb, jnp.zeros((smax, 64), F32)], axis=1)
    cr, sr = _rope_lanes((pos // GRID_W).astype(F32))
    cc, sc = _rope_lanes((pos % GRID_W).astype(F32))
    cosc = jnp.concatenate([cr, cc], axis=1)
    sinc = jnp.concatenate([sr, sc], axis=1)
    (b0, s0), (_, s1) = groups
    t0, q1 = b0 * s0 // tm, s1 // tm

    def pos_block(i):
        return (jnp.where(i < t0, i % (s0 // tm), (i - t0) % q1), 0)

    row = lambda c: pl.BlockSpec((tm, c), lambda i: (i, 0))
    full = lambda a: pl.BlockSpec(a.shape, lambda i: (0,) * a.ndim)
    tab = pl.BlockSpec((tm, LANES), pos_block)
    wd = H_D * DH_D
    widths = (512, 512, 512, H_B * DKP_B, H_B * DKP_B, H_B * DV_B, 512, 256, 256)
    out_shape = [jax.ShapeDtypeStruct((m, c), BF16) for c in widths]
    out_specs = [row(c) for c in widths]
    for _, dil in DILATIONS:
        out_shape += [jax.ShapeDtypeStruct((dil, m // dil, wd), BF16)] * 3
        out_specs += [pl.BlockSpec((dil, tm // dil, wd), lambda i: (0, i, 0))] * 3
    return pl.pallas_call(
        _prep_body,
        out_shape=tuple(out_shape),
        grid=(m // tm,),
        in_specs=[row(N_IN), row(LANES), tab, tab, tab, tab, full(gains256), full(gains512),
                  full(w_uq_p), full(w_ukv)],
        out_specs=tuple(out_specs),
        scratch_shapes=[pltpu.VMEM((H_D, tm, DH_D), F32)] * 4,
        compiler_params=_cparams(("parallel",)),
        name="prepare_mixer_inputs",
    )(proj, kr, cosb, sinb, cosc, sinc, gains256, gains512, w_uq_p, w_ukv)


def _softmax_step_t(k, q, v, m_ref, l_ref, acc_ref, bias=None):
    st = _kq(k, q)
    if bias is not None:
        st = st + bias
    m_prev = m_ref[...]
    m_new = jnp.maximum(m_prev, jnp.max(st, axis=0, keepdims=True))
    alpha = jnp.exp2(m_prev - m_new)
    pt = jnp.exp2(st - m_new)
    l_ref[...] = alpha * l_ref[...] + jnp.sum(pt, axis=0, keepdims=True)
    pv = lax.dot_general(v, pt.astype(BF16), (((0,), (0,)), ((), ())), preferred_element_type=F32)
    acc_ref[...] = alpha * acc_ref[...] + pv
    m_ref[...] = m_new


def _init_stats(m_ref, l_ref, acc_ref):
    m_ref[...] = jnp.full(m_ref.shape, -jnp.inf, F32)
    l_ref[...] = jnp.zeros(l_ref.shape, F32)
    acc_ref[...] = jnp.zeros(acc_ref.shape, F32)


def _kq(k, q):
    return lax.dot_general(k, q, (((1,), (1,)), ((), ())), preferred_element_type=F32)


def _attn_body(q_ref, k_ref, v_ref, o_ref, m_ref, l_ref, acc_ref, *, tk, nk):
    _init_stats(m_ref, l_ref, acc_ref)
    q = q_ref[...]

    def step(j, carry):
        off = pl.multiple_of(j * tk, tk)
        _softmax_step_t(k_ref[pl.ds(off, tk), :], q, v_ref[pl.ds(off, tk), :], m_ref, l_ref, acc_ref)
        return carry

    lax.fori_loop(0, nk, step, 0)
    o_ref[...] = jnp.transpose(acc_ref[...] / l_ref[...]).astype(o_ref.dtype)


def _attn_scratch(tq, dv):
    return [pltpu.VMEM((1, tq), F32), pltpu.VMEM((1, tq), F32), pltpu.VMEM((dv, tq), F32)]


def _attn_tiles(seq):
    return _pick(seq, (ATTN_TQ, 1024, 512, 256, 128)), _pick(seq, (ATTN_TK, 256, 128))


def dense_attention(q, k, v, *, row0, nseq, seq, hq, hk, dk, dv):
    tq, tk = _attn_tiles(seq)
    nq = seq // tq
    group = hq // hk
    qb0 = row0 // tq
    sb0 = row0 // seq
    return pl.pallas_call(
        functools.partial(_attn_body, tk=tk, nk=seq // tk),
        out_shape=jax.ShapeDtypeStruct((nseq * seq, hq * dv), BF16),
        grid=(nseq, hq, nq),
        in_specs=[pl.BlockSpec((tq, dk), lambda b, h, i: (qb0 + b * nq + i, h)),
                  pl.BlockSpec((seq, dk), lambda b, h, i: (sb0 + b, h // group)),
                  pl.BlockSpec((seq, dv), lambda b, h, i: (sb0 + b, h // group))],
        out_specs=pl.BlockSpec((tq, dv), lambda b, h, i: (b * nq + i, h)),
        scratch_shapes=_attn_scratch(tq, dv),
        compiler_params=_cparams(("parallel", "parallel", "parallel")),
        name="dense_attention",
    )(q, k, v)


BIAS_FAR = 1152


def _diff_body(lam_ref, q_ref, k_ref, v_ref, bias_ref, g_ref, o_ref,
               m1, l1, a1, m2, l2, a2, *, tq, tk, nk, lo, out_scale):
    _init_stats(m1, l1, a1)
    _init_stats(m2, l2, a2)
    q = q_ref[...]
    lane = lax.broadcasted_iota(jnp.int32, q.shape, 1)
    zero = jnp.zeros_like(q)
    q1 = jnp.where(lane < DK_A, q, zero)
    q2 = jnp.where(lane >= DK_A, q, zero)
    q0 = pl.program_id(2) * tq

    def step(j, carry):
        off = pl.multiple_of(j * tk, tk)
        k = k_ref[pl.ds(off, tk), :]
        v = v_ref[pl.ds(off, tk), :]
        cols = []
        for c0 in range(0, tq, LANES):
            r0 = jnp.clip(off - q0 - c0, lo, BIAS_FAR) - lo
            cols.append(bias_ref[pl.ds(pl.multiple_of(r0, LANES), tk), :])
        bias = jnp.concatenate(cols, axis=1)
        _softmax_step_t(k, q1, v, m1, l1, a1, bias)
        _softmax_step_t(k, q2, v, m2, l2, a2, bias)
        return carry

    lax.fori_loop(0, nk, step, 0)
    o = a1[...] / l1[...] - lam_ref[0] * (a2[...] / l2[...])
    y = o * lax.rsqrt(jnp.mean(o * o, axis=0, keepdims=True) + NORM_EPS)
    o_ref[...] = jnp.transpose((y * g_ref[...]) * out_scale).astype(o_ref.dtype)


def _rel_bucket(rel):
    half = N_BUCKETS // 2
    exact = half // 2
    n = jnp.abs(rel)
    nf = jnp.maximum(n, 1).astype(F32)
    large = exact + (jnp.log(nf / exact) / math.log(REL_MAX_DIST / exact) * (half - exact)).astype(jnp.int32)
    large = jnp.minimum(large, half - 1)
    return jnp.where(rel > 0, half, 0) + jnp.where(n < exact, n, large)


def _skew(vec, nrows):
    p = vec.shape[-1]
    flat = jnp.tile(vec, (1,) * (vec.ndim - 1) + (nrows,))[..., :nrows * (p - 1)]
    return flat.reshape(vec.shape[:-1] + (nrows, p - 1))


def _diff_bias_geometry(tk):
    lo = -(REL_MAX_DIST + tk)
    return lo, BIAS_FAR - lo + tk


def _diagonal_table(tbl, tk):
    lo, nrows = _diff_bias_geometry(tk)
    rel = jnp.arange(lo - (LANES - 1), lo + nrows + 1, dtype=jnp.int32)
    vec = jnp.transpose(tbl[_rel_bucket(rel)]).astype(F32) * LOG2E
    diag = _skew(vec, LANES)[:, :, LANES - 1:LANES - 1 + nrows]
    return jnp.transpose(diag, (0, 2, 1))


def diff_attention(q, k, v, table, lam, out_g, *, row0, nseq, seq, lam_init):
    tq, tk = _attn_tiles(seq)
    nq = seq // tq
    qb0 = row0 // tq
    sb0 = row0 // seq
    lo, nrows = _diff_bias_geometry(tk)
    assert table.shape == (H_A, nrows, LANES)
    return pl.pallas_call(
        functools.partial(_diff_body, tq=tq, tk=tk, nk=seq // tk, lo=lo, out_scale=1.0 - lam_init),
        out_shape=jax.ShapeDtypeStruct((nseq * seq, H_A * DV_A), BF16),
        grid=(nseq, H_A, nq),
        in_specs=[pl.BlockSpec(memory_space=pltpu.SMEM),
                  pl.BlockSpec((tq, 2 * DK_A), lambda b, h, i: (qb0 + b * nq + i, h)),
                  pl.BlockSpec((seq, 2 * DK_A), lambda b, h, i: (sb0 + b, h)),
                  pl.BlockSpec((seq, DV_A), lambda b, h, i: (sb0 + b, h)),
                  pl.BlockSpec((None, nrows, LANES), lambda b, h, i: (h, 0, 0)),
                  pl.BlockSpec((DV_A, 1), lambda b, h, i: (0, 0))],
        out_specs=pl.BlockSpec((tq, DV_A), lambda b, h, i: (b * nq + i, h)),
        scratch_shapes=_attn_scratch(tq, DV_A) + _attn_scratch(tq, DV_A),
        compiler_params=_cparams(("parallel", "parallel", "parallel")),
        name="diff_attention",
    )(lam, q, k, v, table, out_g.reshape(DV_A, 1).astype(F32))


def _band_body(q_ref, kp_ref, kc_ref, kn_ref, vp_ref, vc_ref, vn_ref, bias_ref, o_ref, lse_ref,
               *, tu, nb):
    n = pl.program_id(2)
    tq = min(tu, 2 * HALO)
    tkk = tq + 2 * HALO
    col = lax.broadcasted_iota(jnp.int32, (1, tkk), 1)
    for h in range(H_D):
        hs = slice(h * DH_D, (h + 1) * DH_D)
        k = jnp.concatenate([kp_ref[:, hs], kc_ref[:, hs], kn_ref[:, hs]], axis=0)
        v = jnp.concatenate([vp_ref[:, hs], vc_ref[:, hs], vn_ref[:, hs]], axis=0)
        for q0 in range(0, tu, tq):
            s = _kq(q_ref[q0:q0 + tq, hs], k[q0:q0 + tkk]) + bias_ref[h, 0:tq, 0:tkk]
            if q0 == 0:
                s = jnp.where((col < HALO) & (n == 0), NEG_INF, s)
            if q0 + tq == tu:
                s = jnp.where((col >= tq + HALO) & (n == nb - 1), NEG_INF, s)
            m = jnp.max(s, axis=1, keepdims=True)
            p = jnp.exp2(s - m)
            l = jnp.sum(p, axis=1, keepdims=True)
            o = jnp.dot(p.astype(BF16), v[q0:q0 + tkk], preferred_element_type=F32) / l
            o_ref[q0:q0 + tq, hs] = o.astype(o_ref.dtype)
            lse_ref[q0:q0 + tq, hs] = jnp.broadcast_to(m + jnp.log2(l), (tq, DH_D))


def _band_table(bias_d, tu):
    p = 2 * tu + 2 * HALO + 1
    x = jnp.arange(p, dtype=jnp.int32)
    vecs = []
    for g, (_, dil) in enumerate(DILATIONS):
        tbl = bias_d[:, g * H_D:(g + 1) * H_D]
        band = jnp.transpose(tbl[_rel_bucket((x - HALO) * dil)]).astype(F32) * LOG2E
        vecs.append(jnp.where((x <= 2 * HALO)[None], band, NEG_INF))
    return _skew(jnp.stack(vecs), tu)[..., :tu + 2 * HALO]


def band_attention(q, k, v, table, g, *, row0, nseq, seq):
    dil, rows, width = q.shape
    length = seq // dil
    tu = table.shape[2]
    nb = length // tu
    u0 = row0 // dil
    hpt = tu // HALO
    last_halo = rows // HALO - 1

    def tile(b, n):
        return u0 // tu + b * nb + n

    cur = pl.BlockSpec((None, tu, width), lambda b, r, n: (r, tile(b, n), 0))
    prev = pl.BlockSpec((None, HALO, width), lambda b, r, n: (r, jnp.maximum(tile(b, n) * hpt - 1, 0), 0))
    nxt = pl.BlockSpec((None, HALO, width),
                       lambda b, r, n: (r, jnp.minimum((tile(b, n) + 1) * hpt, last_halo), 0))
    out_rows = nseq * length
    out_spec = pl.BlockSpec((None, tu, width), lambda b, r, n: (r, b * nb + n, 0))
    return pl.pallas_call(
        functools.partial(_band_body, tu=tu, nb=nb),
        out_shape=(jax.ShapeDtypeStruct((dil, out_rows, width), F32),
                   jax.ShapeDtypeStruct((dil, out_rows, width), F32)),
        grid=(nseq, dil, nb),
        in_specs=[cur, prev, cur, nxt, prev, cur, nxt,
                  pl.BlockSpec((None, H_D, tu, tu + 2 * HALO), lambda b, r, n: (g, 0, 0, 0))],
        out_specs=(out_spec, out_spec),
        compiler_params=_cparams(("parallel", "parallel", "parallel")),
        name="band_attention",
    )(q, k, k, k, v, v, v, table)


def _mix_body(*refs):
    o_refs = refs[:N_DIL]
    l_refs = refs[N_DIL:2 * N_DIL]
    out_ref = refs[2 * N_DIL]
    scratch = refs[2 * N_DIL + 1:]
    tm = out_ref.shape[0]
    vals = [o_refs[0][0], l_refs[0][0]]
    for g, (_, dil) in enumerate(DILATIONS[1:], start=1):
        for src, dst in ((o_refs[g], scratch[2 * g - 2]), (l_refs[g], scratch[2 * g - 1])):
            for h in range(H_D):
                for r in range(dil):
                    dst[h, pl.ds(r, tm // dil, stride=dil), :] = src[r, :, h * DH_D:(h + 1) * DH_D]
            vals.append(jnp.concatenate([dst[h] for h in range(H_D)], axis=1))
    outs, ls = vals[0::2], vals[1::2]
    mx = functools.reduce(jnp.maximum, ls)
    es = [jnp.exp2(l - mx) for l in ls]
    den = functools.reduce(lambda a, b: a + b, es)
    acc = None
    for e, o in zip(es, outs):
        term = (e / den) * o
        acc = term if acc is None else acc + term
    out_ref[...] = acc.astype(out_ref.dtype)


def band_mixture(outs, lses):
    _, m, c = outs[0].shape
    tm = _pick(m, (PREP_TM, 128, 64))
    specs = [pl.BlockSpec((dil, tm // dil, c), lambda i: (0, i, 0)) for _, dil in DILATIONS]
    return pl.pallas_call(
        _mix_body,
        out_shape=jax.ShapeDtypeStruct((m, c), BF16),
        grid=(m // tm,),
        in_specs=specs * 2,
        out_specs=pl.BlockSpec((tm, c), lambda i: (i, 0)),
        scratch_shapes=[pltpu.VMEM((H_D, tm, DH_D), F32)] * (2 * (N_DIL - 1)),
        compiler_params=_cparams(("parallel",)),
        name="band_mixture",
    )(*outs, *lses)


def kernel(x_prompt, x_sample, p_prompt, p_sample, rel_bias, norm_mix, w_in, a_q_norm, a_k_norm, a_lambda_q1, a_lambda_k1, a_lambda_q2, a_lambda_k2, a_out_norm, b_cq_norm, b_ckv_norm, b_w_uq, b_w_ukv, b_q_norm, b_k_norm, c_q_norm, c_k_norm, d_q_norm, d_k_norm, w_gate, w_branch, w_out, norm_ffn, w_ff1, w_ff2, norm_ple, w_ple_gate, w_ple_proj):
    depth = w_in.shape[0]
    d_model = x_prompt.shape[-1]
    groups = [(x_prompt.shape[0], x_prompt.shape[1]), (x_sample.shape[0], x_sample.shape[1])]
    row0s = [0, groups[0][0] * groups[0][1]]
    x = [x_prompt.reshape(-1, d_model), x_sample.reshape(-1, d_model)]
    p_parts = [p_prompt.reshape(depth, -1, p_prompt.shape[-1]), p_sample.reshape(depth, -1, p_sample.shape[-1])]

    in_sizes = (512, 512, 512, 512, 512, 64, 512, 256, 256, 1536, 512, 512)
    offs = np.concatenate([[0], np.cumsum(in_sizes)])
    seg = lambda w, a, b: w[:, offs[a]:offs[b]]

    table_a = {}
    tables_d = {}
    for _, seq in groups:
        tk = _attn_tiles(seq)[1]
        if tk not in table_a:
            table_a[tk] = _diagonal_table(rel_bias[:, :H_A], tk)
        for _, dil in DILATIONS:
            tu = min(BAND_TU, seq // dil)
            if tu not in tables_d:
                tables_d[tu] = _band_table(rel_bias[:, H_A:], tu)

    two = lambda g: jnp.concatenate([g, g])
    pad_b = lambda g: jnp.pad(g, (0, DKP_B - DN_B - DR_B))
    h = rmsnorm(x, norm_mix[0])
    for i in range(depth):
        lam_init = 0.8 - 0.6 * math.exp(-0.3 * i)
        wi = w_in[i]
        w_in_p = jnp.concatenate([seg(wi, 0, 3), seg(wi, 6, 9), seg(wi, 9, 12), seg(wi, 3, 5)], axis=1).astype(BF16)
        w_kr_p = jnp.pad(seg(wi, 5, 6), ((0, 0), (0, LANES - DR_B))).astype(BF16)
        w_uq_p = jnp.pad(b_w_uq[i].reshape(Q_RANK_B, H_B, DN_B + DR_B),
                         ((0, 0), (0, 0), (0, DKP_B - DN_B - DR_B))).reshape(Q_RANK_B, H_B * DKP_B).astype(BF16)
        zeros = jnp.zeros((PAIR,), F32)
        gains256 = jnp.stack([two(two(a_q_norm[i])) * (DK_A ** -0.5 * LOG2E), two(two(a_k_norm[i])),
                              two(c_q_norm[i]) * (DH_C ** -0.5 * LOG2E), two(c_k_norm[i]),
                              two(d_q_norm[i]) * (DH_D ** -0.5 * LOG2E), two(d_k_norm[i]), zeros, zeros]).astype(F32)
        gains512 = jnp.stack([b_cq_norm[i], b_ckv_norm[i],
                              jnp.concatenate([pad_b(b_q_norm[i]) * ((DN_B + DR_B) ** -0.5 * LOG2E),
                                               pad_b(b_k_norm[i])])]
                             + [jnp.zeros((512,), F32)] * 5).astype(F32)
        lam = (jnp.exp(jnp.sum(a_lambda_q1[i].astype(F32) * a_lambda_k1[i].astype(F32)))
               - jnp.exp(jnp.sum(a_lambda_q2[i].astype(F32) * a_lambda_k2[i].astype(F32))) + lam_init)
        lam = lam.reshape(1).astype(F32)

        proj = matmul(h, w_in_p, out_dtype=F32)
        kr = matmul(h, w_kr_p, out_dtype=F32)
        (qa, ka, va, qb, kb, vb, qc, kc, vc, *qkv_d) = prepare_mixer_inputs(
            proj, kr, groups, gains256, gains512, w_uq_p, b_w_ukv[i].astype(BF16))

        branches = [[], [], [], []]
        for (nseq, seq), row0 in zip(groups, row0s):
            kw = dict(row0=row0, nseq=nseq, seq=seq)
            branches[0].append(diff_attention(qa, ka, va, table_a[_attn_tiles(seq)[1]], lam, a_out_norm[i],
                                              lam_init=lam_init, **kw))
            branches[1].append(dense_attention(qb, kb, vb, hq=H_B, hk=H_B, dk=DKP_B, dv=DV_B, **kw))
            branches[2].append(dense_attention(qc, kc, vc, hq=H_C, hk=KV_C, dk=DH_C, dv=DH_C, **kw))
            outs, lses = zip(*[band_attention(*qkv_d[3 * g:3 * g + 3], tables_d[min(BAND_TU, seq // dil)], g, **kw)
                               for g, (_, dil) in enumerate(DILATIONS)])
            branches[3].append(band_mixture(outs, lses))

        merged = gated_merge(h, w_gate[i].astype(BF16), w_branch[i].astype(BF16), branches, out_dtype=BF16)
        x, h2 = matmul(merged, w_out[i].astype(BF16), out_dtype=F32, res=x, norm_gain=norm_ffn[i])
        u = matmul(h2, w_ff1[i].astype(BF16), out_dtype=BF16, act="relu2")
        x = matmul(u, w_ff2[i].astype(BF16), out_dtype=F32, res=x, tn=256, tk=u.shape[1])
        ple = dict(out_dtype=F32, res=x)
        ple_w = (norm_ple[i], w_ple_gate[i].astype(BF16)[None], w_ple_proj[i].astype(BF16)[None])
        if i + 1 < depth:
            x, h = gated_merge(*ple_w, [[p[i] for p in p_parts]], norm_gain=norm_mix[i + 1], **ple)
        else:
            ys = [gated_merge(*ple_w, [[p[i]]], full_rows=True, row0=row0, nrows=nseq * seq, **ple)
                  for p, (nseq, seq), row0 in zip(p_parts, groups, row0s)]

    y_prompt, y_sample = ys[0].reshape(x_prompt.shape), ys[1].reshape(x_sample.shape)
    return (y_prompt, y_sample)
```
